```python
import math
import jax
import jax.numpy as jnp
from jax import lax
import numpy as np

D_MODEL = 2048
BATCH = 1
SEQ = 16384
DEPTH = 2

N_HEADS = 16
HEAD_DIM = D_MODEL // N_HEADS
BRANCH = N_HEADS * HEAD_DIM
PLE_DIM = 256
ROPE_THETA = 10000.0
SB_Q_BLOCK = 128
MOBA_BLOCK = 256
MOBA_TOPK = 3
MOBA_Q_CHUNK = 64
N_A = DEPTH // 2
N_B = DEPTH - N_A
DEEPNORM_ALPHA = (2.0 * DEPTH) ** 0.25
DEEPNORM_BETA = (8.0 * DEPTH) ** -0.25
LN_EPS = 1e-5
NEG_BIG = -1e30

kernel_name = 'yoco_stickbreak_moba_hybrid'


def layer_norm(x, g, b):
    xf = x.astype(jnp.float32)
    mu = jnp.mean(xf, axis=-1, keepdims=True)
    var = jnp.mean(jnp.square(xf - mu), axis=-1, keepdims=True)
    return ((xf - mu) * lax.rsqrt(var + LN_EPS)).astype(x.dtype) * g + b


def split_heads(t):
    b, s, _ = t.shape
    return t.reshape(b, s, N_HEADS, HEAD_DIM).transpose(0, 2, 1, 3)


def merge_heads(t):
    b, h, s, d = t.shape
    return t.transpose(0, 2, 1, 3).reshape(b, s, h * d)


def rope(t, pos):
    half = HEAD_DIM // 2
    inv_freq = ROPE_THETA ** (-jnp.arange(half, dtype=jnp.float32) / half)
    ang = pos.astype(jnp.float32)[:, None] * inv_freq[None, :]
    cos = jnp.cos(ang).astype(t.dtype)
    sin = jnp.sin(ang).astype(t.dtype)
    t1, t2 = t[..., :half], t[..., half:]
    return jnp.concatenate([t1 * cos - t2 * sin, t2 * cos + t1 * sin], axis=-1)


def stick_breaking_attention(q, k, v):
    b, h, s, dh = q.shape
    n_blk = s // SB_Q_BLOCK
    scale = 1.0 / math.sqrt(dh)
    qb = q.reshape(b, h, n_blk, SB_Q_BLOCK, dh).transpose(2, 0, 1, 3, 4)
    key_pos = jnp.arange(s)

    def block(args):
        qi, i = args
        t_idx = i * SB_Q_BLOCK + jnp.arange(SB_Q_BLOCK)
        z = jnp.einsum('bhqd,bhkd->bhqk', qi, k).astype(jnp.float32) * scale
        past = key_pos[None, :] < t_idx[:, None]
        log_beta = jax.nn.log_sigmoid(z)
        log_1m_beta = jnp.where(past, jax.nn.log_sigmoid(-z), 0.0)
        tail = lax.cumsum(log_1m_beta, axis=3, reverse=True) - log_1m_beta
        w = jnp.where(past, jnp.exp(log_beta + tail), 0.0)
        return jnp.einsum('bhqk,bhkd->bhqd', w.astype(v.dtype), v)

    out = lax.map(block, (qb, jnp.arange(n_blk)))
    return out.transpose(1, 2, 0, 3, 4).reshape(b, h, s, dh)


def moba_attention(q, k, v, k_mean):
    b, h, s, dh = q.shape
    sp = k.shape[2]
    nb = sp // MOBA_BLOCK
    k_top = min(MOBA_TOPK, nb)
    scale = 1.0 / math.sqrt(dh)
    qp = jnp.pad(q, ((0, 0), (0, 0), (0, sp - s), (0, 0)))
    kb = k.reshape(b, h, nb, MOBA_BLOCK, dh)
    vb = v.reshape(b, h, nb, MOBA_BLOCK, dh)
    n_chunks = sp // MOBA_Q_CHUNK
    qc = qp.reshape(b, h, n_chunks, MOBA_Q_CHUNK, dh).transpose(2, 0, 1, 3, 4)
    b_idx = jnp.arange(b)[:, None, None, None]
    h_idx = jnp.arange(h)[None, :, None, None]
    blk_ids = jnp.arange(nb)
    rank = jnp.arange(k_top)
    key_off = jnp.arange(MOBA_BLOCK)
    n_sel = k_top * MOBA_BLOCK

    def chunk(args):
        qi, c = args
        t_idx = c * MOBA_Q_CHUNK + jnp.arange(MOBA_Q_CHUNK)
        own = (c * MOBA_Q_CHUNK) // MOBA_BLOCK
        gate = jnp.einsum('bhqd,bhnd->bhqn', qi, k_mean).astype(jnp.float32)
        gate = jnp.where(blk_ids < own, gate, NEG_BIG)
        _, sel = lax.top_k(gate, k_top)
        sel_ok = rank < own
        k_sel = kb[b_idx, h_idx, sel]
        v_sel = vb[b_idx, h_idx, sel]
        s_sel = jnp.einsum('bhqd,bhqnld->bhqnl', qi, k_sel).astype(jnp.float32) * scale
        s_sel = jnp.where(sel_ok[:, None], s_sel, NEG_BIG)
        k_own = lax.dynamic_index_in_dim(kb, own, axis=2, keepdims=False)
        v_own = lax.dynamic_index_in_dim(vb, own, axis=2, keepdims=False)
        s_own = jnp.einsum('bhqd,bhld->bhql', qi, k_own).astype(jnp.float32) * scale
        own_pos = own * MOBA_BLOCK + key_off
        s_own = jnp.where(own_pos[None, :] <= t_idx[:, None], s_own, NEG_BIG)
        scores = jnp.concatenate([s_sel.reshape(b, h, MOBA_Q_CHUNK, n_sel), s_own], axis=-1)
        probs = jax.nn.softmax(scores, axis=-1).astype(v.dtype)
        p_sel = probs[..., :n_sel].reshape(b, h, MOBA_Q_CHUNK, k_top, MOBA_BLOCK)
        p_own = probs[..., n_sel:]
        return (jnp.einsum('bhqnl,bhqnld->bhqd', p_sel, v_sel)
                + jnp.einsum('bhql,bhld->bhqd', p_own, v_own))

    out = lax.map(chunk, (qc, jnp.arange(n_chunks)))
    out = out.transpose(1, 2, 0, 3, 4).reshape(b, h, sp, dh)
    return out[:, :, :s]


def stick_breaking_layer(x, w_in, w_out):
    q, k, v, z = jnp.split(x @ w_in, 4, axis=-1)
    o = stick_breaking_attention(split_heads(q), split_heads(k), split_heads(v))
    return (merge_heads(o) * jax.nn.silu(z)) @ w_out


def shared_kv(x, ln_g, ln_b, w_kv, pos):
    xs = layer_norm(x, ln_g, ln_b)
    k, v = jnp.split(xs @ w_kv, 2, axis=-1)
    k = rope(split_heads(k), pos)
    v = split_heads(v)
    s = x.shape[1]
    sp = -(-s // MOBA_BLOCK) * MOBA_BLOCK
    pad = ((0, 0), (0, 0), (0, sp - s), (0, 0))
    k = jnp.pad(k, pad)
    v = jnp.pad(v, pad)
    b, h, _, dh = k.shape
    k_mean = k.reshape(b, h, sp // MOBA_BLOCK, MOBA_BLOCK, dh).mean(axis=3)
    return k, v, k_mean


def moba_layer(x, w_in, w_out, k, v, k_mean, pos):
    q, z = jnp.split(x @ w_in, 2, axis=-1)
    o = moba_attention(rope(split_heads(q), pos), k, v, k_mean)
    return (merge_heads(o) * jax.nn.silu(z)) @ w_out


def setup_inputs(seed: int = 0) -> dict:
    key = jax.random.key(seed)
    ks = jax.random.split(key, 13)
    f32 = jnp.float32

    def nrm(k, shape, scale):
        return jax.random.normal(k, shape, f32) * scale

    return {
        'x': nrm(ks[0], (BATCH, SEQ, D_MODEL), 1.0),
        'p': nrm(ks[1], (DEPTH, BATCH, SEQ, PLE_DIM), 1.0),
        'w_in_a': nrm(ks[2], (N_A, D_MODEL, 4 * BRANCH), D_MODEL ** -0.5),
        'w_out_a': nrm(ks[3], (N_A, BRANCH, D_MODEL), BRANCH ** -0.5 * DEEPNORM_BETA),
        'w_kv': nrm(ks[4], (D_MODEL, 2 * BRANCH), D_MODEL ** -0.5),
        'ln_kv_g': 1.0 + nrm(ks[5], (D_MODEL,), 0.02),
        'ln_kv_b': nrm(ks[6], (D_MODEL,), 0.02),
        'w_in_b': nrm(ks[7], (N_B, D_MODEL, 2 * BRANCH), D_MODEL ** -0.5),
        'w_out_b': nrm(ks[8], (N_B, BRANCH, D_MODEL), BRANCH ** -0.5 * DEEPNORM_BETA),
        'ln_g': 1.0 + nrm(ks[9], (DEPTH, D_MODEL), 0.02),
        'ln_b': nrm(ks[10], (DEPTH, D_MODEL), 0.02),
        'w_ple': nrm(ks[11], (DEPTH, PLE_DIM, D_MODEL), PLE_DIM ** -0.5),
        'w_ple_gate': nrm(ks[12], (DEPTH, D_MODEL, D_MODEL), D_MODEL ** -0.5),
    }


def reference(x, p, w_in_a, w_out_a, w_kv, ln_kv_g, ln_kv_b, w_in_b, w_out_b,
              ln_g, ln_b, w_ple, w_ple_gate):
    pos = jnp.arange(x.shape[1], dtype=jnp.int32)
    k_sh = v_sh = km_sh = None
    for i in range(DEPTH):
        if i < N_A:
            y = stick_breaking_layer(x, w_in_a[i], w_out_a[i])
        else:
            if i == N_A:
                k_sh, v_sh, km_sh = shared_kv(x, ln_kv_g, ln_kv_b, w_kv, pos)
            y = moba_layer(x, w_in_b[i - N_A], w_out_b[i - N_A], k_sh, v_sh, km_sh, pos)
        x = layer_norm(DEEPNORM_ALPHA * x + y, ln_g[i], ln_b[i])
        x = x + (p[i] @ w_ple[i]) * jax.nn.sigmoid(x @ w_ple_gate[i])
    return x
```

```python
import functools
import math

import jax
import jax.numpy as jnp
from jax import lax
from jax.experimental import pallas as pl
from jax.experimental.pallas import tpu as pltpu

N_HEADS = 16
HEAD_DIM = 128
ROPE_THETA = 10000.0
MOBA_BLOCK = 256
MOBA_TOPK = 3
LN_EPS = 1e-5
NEG_BIG = -1e30

LANES = 128
MXU_DTYPE = jnp.bfloat16
VMEM_LIMIT_BYTES = 60 * 1024 * 1024

SB_SKIP_LOG = 104.0

_NT = (((1,), (1,)), ((), ()))


def _layer_norm(x, g, b):
    mu = jnp.mean(x, axis=-1, keepdims=True)
    xc = x - mu
    var = jnp.mean(xc * xc, axis=-1, keepdims=True)
    return xc * lax.rsqrt(var + LN_EPS) * g + b


def _silu(z):
    return z * jax.nn.sigmoid(z)


def _proj_kernel(*refs, ln, rope_tiles, with_kmean, tm, tn):
    refs = list(refs)
    x_ref = refs.pop(0)
    if ln:
        g_ref = refs.pop(0)
        b_ref = refs.pop(0)
    if rope_tiles:
        cos_ref = refs.pop(0)
        sin_ref = refs.pop(0)
    w_ref = refs.pop(0)
    o_ref = refs.pop(0)
    if with_kmean:
        km_ref = refs.pop(0)
    a_scr = refs.pop(0)

    n = pl.program_id(1)

    @pl.when(n == 0)
    def _():
        xv = x_ref[...]
        if ln:
            xv = _layer_norm(xv, g_ref[...], b_ref[...])
        a_scr[...] = xv.astype(a_scr.dtype)

    acc = jnp.dot(a_scr[...], w_ref[...], preferred_element_type=jnp.float32)

    def write_plain():
        o_ref[...] = acc.astype(o_ref.dtype)
        if with_kmean:
            km_ref[...] = jnp.zeros(km_ref.shape, km_ref.dtype)

    def write_rope():
        cos2 = cos_ref[...]
        sin2 = sin_ref[...]
        for j in range(tn // HEAD_DIM):
            t = acc[:, j * HEAD_DIM:(j + 1) * HEAD_DIM]
            r = t * cos2 + pltpu.roll(t, HEAD_DIM // 2, axis=1) * sin2
            o_ref[:, j * HEAD_DIM:(j + 1) * HEAD_DIM] = r.astype(o_ref.dtype)
            if with_kmean:
                for blk in range(tm // MOBA_BLOCK):
                    rows = r[blk * MOBA_BLOCK:(blk + 1) * MOBA_BLOCK, :]
                    km_ref[blk, :, j * HEAD_DIM:(j + 1) * HEAD_DIM] = (
                        jnp.sum(rows, axis=0, keepdims=True) * (1.0 / MOBA_BLOCK))

    if rope_tiles:
        pl.when(n < rope_tiles)(write_rope)
        pl.when(n >= rope_tiles)(write_plain)
    else:
        write_plain()


def _proj(x, w, *, ln=None, rope=None, rope_cols=0, with_kmean=False, tm=1024, tn=512):
    s, d = x.shape
    n_out = w.shape[1]
    tm = min(tm, s)
    assert s % tm == 0 and n_out % tn == 0 and tn % HEAD_DIM == 0 and rope_cols % tn == 0
    assert tm % MOBA_BLOCK == 0
    rope_tiles = rope_cols // tn
    in_specs = [pl.BlockSpec((tm, d), lambda m, n: (m, 0))]
    args = [x]
    if ln is not None:
        in_specs += [pl.BlockSpec((1, d), lambda m, n: (0, 0))] * 2
        args += [ln[0].reshape(1, d), ln[1].reshape(1, d)]
    if rope_tiles:
        in_specs += [pl.BlockSpec((tm, HEAD_DIM), lambda m, n: (m, 0))] * 2
        args += [rope[0], rope[1]]
    in_specs.append(pl.BlockSpec((d, tn), lambda m, n: (0, n)))
    args.append(w)
    out_shape = [jax.ShapeDtypeStruct((s, n_out), MXU_DTYPE)]
    out_specs = [pl.BlockSpec((tm, tn), lambda m, n: (m, n))]
    if with_kmean:
        out_shape.append(jax.ShapeDtypeStruct((s // MOBA_BLOCK, 1, n_out), jnp.float32))
        out_specs.append(pl.BlockSpec((tm // MOBA_BLOCK, 1, tn), lambda m, n: (m, 0, n)))
    res = pl.pallas_call(
        functools.partial(_proj_kernel, ln=ln is not None, rope_tiles=rope_tiles,
                          with_kmean=with_kmean, tm=tm, tn=tn),
        grid=(s // tm, n_out // tn),
        in_specs=in_specs,
        out_specs=out_specs,
        out_shape=out_shape,
        scratch_shapes=[pltpu.VMEM((tm, d), MXU_DTYPE)],
        compiler_params=pltpu.CompilerParams(
            dimension_semantics=("arbitrary", "arbitrary"),
            vmem_limit_bytes=VMEM_LIMIT_BYTES),
        name="proj",
    )(*args)
    return res if with_kmean else res[0]


def _sb_kernel(q_ref, k_ref, v_ref, z_ref, o_ref, *, t, scale):
    i = pl.program_id(1)
    q = q_ref[...]
    row = lax.broadcasted_iota(jnp.int32, (t, t), 0)
    col = lax.broadcasted_iota(jnp.int32, (t, t), 1)
    tri = (row > col).astype(MXU_DTYPE)
    past = col < row

    def tile(kb, c, acc, diag):
        start = pl.multiple_of(kb * t, t)
        ks = k_ref[pl.ds(start, t), :]
        vs = v_ref[pl.ds(start, t), :]
        s = lax.dot_general(q, ks, _NT, preferred_element_type=jnp.float32) * scale
        sp = jnp.log1p(jnp.exp(-jnp.abs(s)))
        log_beta = jnp.minimum(s, 0.0) - sp
        l1m = -jnp.maximum(s, 0.0) - sp
        if diag:
            l1m = jnp.where(past, l1m, 0.0)
        hi = l1m.astype(MXU_DTYPE)
        lo = (l1m - hi.astype(jnp.float32)).astype(MXU_DTYPE)
        tail = (jnp.dot(hi, tri, preferred_element_type=jnp.float32)
                + jnp.dot(lo, tri, preferred_element_type=jnp.float32) + c)
        w = jnp.exp(log_beta + tail)
        if diag:
            w = jnp.where(past, w, 0.0)
        acc = acc + jnp.dot(w.astype(MXU_DTYPE), vs, preferred_element_type=jnp.float32)
        c = c + jnp.sum(l1m, axis=1, keepdims=True)
        return c, acc

    c0 = jnp.zeros((t, 1), jnp.float32)
    acc0 = jnp.zeros((t, HEAD_DIM), jnp.float32)
    c, acc = tile(i, c0, acc0, True)

    def cond(carry):
        kb, c, _ = carry
        return jnp.logical_and(kb >= 0, jnp.max(c) > -SB_SKIP_LOG)

    def body(carry):
        kb, c, acc = carry
        c, acc = tile(kb, c, acc, False)
        return kb - 1, c, acc

    _, _, acc = lax.while_loop(cond, body, (i - 1, c, acc))
    o_ref[...] = (acc * _silu(z_ref[...].astype(jnp.float32))).astype(o_ref.dtype)


def _sb_attention(qkvz, *, t=256):
    s = qkvz.shape[0]
    d = N_HEADS * HEAD_DIM
    assert s % t == 0
    h_ = N_HEADS
    return pl.pallas_call(
        functools.partial(_sb_kernel, t=t, scale=1.0 / math.sqrt(HEAD_DIM)),
        grid=(h_, s // t),
        in_specs=[
            pl.BlockSpec((t, HEAD_DIM), lambda h, i: (i, h)),
            pl.BlockSpec((s, HEAD_DIM), lambda h, i: (0, h_ + h)),
            pl.BlockSpec((s, HEAD_DIM), lambda h, i: (0, 2 * h_ + h)),
            pl.BlockSpec((t, HEAD_DIM), lambda h, i: (i, 3 * h_ + h)),
        ],
        out_specs=pl.BlockSpec((t, HEAD_DIM), lambda h, i: (i, h)),
        out_shape=jax.ShapeDtypeStruct((s, d), MXU_DTYPE),
        compiler_params=pltpu.CompilerParams(
            dimension_semantics=("arbitrary", "arbitrary"),
            vmem_limit_bytes=VMEM_LIMIT_BYTES),
        name="sb_attention",
    )(qkvz, qkvz, qkvz, qkvz)


def _moba_kernel(q_ref, z_ref, k_ref, v_ref, km_ref, o_ref, *, scale):
    t = MOBA_BLOCK
    i = pl.program_id(1)
    q = q_ref[...]
    nbp = km_ref.shape[0]

    gate = lax.dot_general(q, km_ref[...].astype(MXU_DTYPE), _NT,
                           preferred_element_type=jnp.float32)
    blk = lax.broadcasted_iota(jnp.int32, (t, nbp), 1)
    valid = blk < i
    g = jnp.where(valid, gate, NEG_BIG)
    sel = jnp.zeros((t, nbp), jnp.float32)
    for _ in range(MOBA_TOPK):
        mx = jnp.max(g, axis=1, keepdims=True)
        first = jnp.min(jnp.where(g == mx, blk, nbp), axis=1, keepdims=True)
        pick = blk == first
        sel = jnp.where(pick, 1.0, sel)
        g = jnp.where(pick, -jnp.inf, g)
    sel = jnp.where(valid, sel, 0.0)

    row = lax.broadcasted_iota(jnp.int32, (t, t), 0)
    col = lax.broadcasted_iota(jnp.int32, (t, t), 1)
    own = pl.multiple_of(i * t, t)
    s = lax.dot_general(q, k_ref[pl.ds(own, t), :], _NT,
                        preferred_element_type=jnp.float32) * scale
    s = jnp.where(col <= row, s, NEG_BIG)
    m = jnp.max(s, axis=1, keepdims=True)
    p = jnp.exp(s - m)
    l = jnp.sum(p, axis=1, keepdims=True)
    acc = jnp.dot(p.astype(MXU_DTYPE), v_ref[pl.ds(own, t), :],
                  preferred_element_type=jnp.float32)

    def body(kb, carry):
        m, l, acc = carry
        start = pl.multiple_of(kb * t, t)
        picked = jnp.sum(jnp.where(blk == kb, sel, 0.0), axis=1, keepdims=True) > 0.0
        s = lax.dot_general(q, k_ref[pl.ds(start, t), :], _NT,
                            preferred_element_type=jnp.float32) * scale
        s = jnp.where(picked, s, NEG_BIG)
        m_new = jnp.maximum(m, jnp.max(s, axis=1, keepdims=True))
        a = jnp.exp(m - m_new)
        p = jnp.exp(s - m_new)
        l = a * l + jnp.sum(p, axis=1, keepdims=True)
        acc = a * acc + jnp.dot(p.astype(MXU_DTYPE), v_ref[pl.ds(start, t), :],
                                preferred_element_type=jnp.float32)
        return m_new, l, acc

    m, l, acc = lax.fori_loop(0, i, body, (m, l, acc))
    o_ref[...] = (acc / l * _silu(z_ref[...].astype(jnp.float32))).astype(o_ref.dtype)


def _moba_attention(qz, kv, km):
    s = qz.shape[0]
    d = N_HEADS * HEAD_DIM
    t = MOBA_BLOCK
    h_ = N_HEADS
    nbp = km.shape[0]
    return pl.pallas_call(
        functools.partial(_moba_kernel, scale=1.0 / math.sqrt(HEAD_DIM)),
        grid=(h_, s // t),
        in_specs=[
            pl.BlockSpec((t, HEAD_DIM), lambda h, i: (i, h)),
            pl.BlockSpec((t, HEAD_DIM), lambda h, i: (i, h_ + h)),
            pl.BlockSpec((s, HEAD_DIM), lambda h, i: (0, h)),
            pl.BlockSpec((s, HEAD_DIM), lambda h, i: (0, h_ + h)),
            pl.BlockSpec((nbp, HEAD_DIM), lambda h, i: (0, h)),
        ],
        out_specs=pl.BlockSpec((t, HEAD_DIM), lambda h, i: (i, h)),
        out_shape=jax.ShapeDtypeStruct((s, d), MXU_DTYPE),
        compiler_params=pltpu.CompilerParams(
            dimension_semantics=("arbitrary", "arbitrary"),
            vmem_limit_bytes=VMEM_LIMIT_BYTES),
        name="moba_attention",
    )(qz, qz, kv, kv, km)


def _post_kernel(a_ref, x_ref, p_ref, wo_ref, g_ref, b_ref, wp_ref, wg_ref, o_ref, *, alpha):
    y = jnp.dot(a_ref[...], wo_ref[...], preferred_element_type=jnp.float32)
    xn = _layer_norm(alpha * x_ref[...] + y, g_ref[...], b_ref[...])
    gate = jax.nn.sigmoid(jnp.dot(xn.astype(MXU_DTYPE), wg_ref[...],
                                  preferred_element_type=jnp.float32))
    e = jnp.dot(p_ref[...].astype(MXU_DTYPE), wp_ref[...], preferred_element_type=jnp.float32)
    o_ref[...] = xn + e * gate


def _post(a, x, p, w_out, g, b, w_ple, w_gate, *, alpha, tm=256):
    s, d = x.shape
    pd = p.shape[1]
    tm = min(tm, s)
    assert s % tm == 0

    def const(shape):
        return pl.BlockSpec(shape, lambda m: (0, 0), pipeline_mode=pl.Buffered(1))

    return pl.pallas_call(
        functools.partial(_post_kernel, alpha=alpha),
        grid=(s // tm,),
        in_specs=[
            pl.BlockSpec((tm, d), lambda m: (m, 0)),
            pl.BlockSpec((tm, d), lambda m: (m, 0)),
            pl.BlockSpec((tm, pd), lambda m: (m, 0)),
            const((d, d)),
            const((1, d)),
            const((1, d)),
            const((pd, d)),
            const((d, d)),
        ],
        out_specs=pl.BlockSpec((tm, d), lambda m: (m, 0)),
        out_shape=jax.ShapeDtypeStruct((s, d), jnp.float32),
        compiler_params=pltpu.CompilerParams(
            dimension_semantics=("arbitrary",),
            vmem_limit_bytes=VMEM_LIMIT_BYTES),
        name="post",
    )(a, x, p, w_out, g.reshape(1, d), b.reshape(1, d), w_ple, w_gate)


def _rope_tables(s):
    half = HEAD_DIM // 2
    inv_freq = ROPE_THETA ** (-jnp.arange(half, dtype=jnp.float32) / half)
    ang = jnp.arange(s, dtype=jnp.int32).astype(jnp.float32)[:, None] * inv_freq[None, :]
    cos = jnp.cos(ang)
    sin = jnp.sin(ang)
    return jnp.concatenate([cos, cos], axis=1), jnp.concatenate([-sin, sin], axis=1)


def kernel(x, p, w_in_a, w_out_a, w_kv, ln_kv_g, ln_kv_b, w_in_b, w_out_b, ln_g, ln_b,
           w_ple, w_ple_gate):
    bsz, s, d = x.shape
    depth = p.shape[0]
    n_a = w_in_a.shape[0]
    assert d == N_HEADS * HEAD_DIM and s % MOBA_BLOCK == 0
    alpha = (2.0 * depth) ** 0.25
    cast = lambda w: w.astype(MXU_DTYPE)
    rope = _rope_tables(s)
    nbp = -(-(s // MOBA_BLOCK) // LANES) * LANES

    outs = []
    for bi in range(bsz):
        xb = x[bi]
        kv = km = None
        for i in range(depth):
            if i < n_a:
                qkvz = _proj(xb, cast(w_in_a[i]))
                a = _sb_attention(qkvz)
                w_out = w_out_a[i]
            else:
                if i == n_a:
                    kv, km3 = _proj(xb, cast(w_kv), ln=(ln_kv_g, ln_kv_b), rope=rope,
                                    rope_cols=d, with_kmean=True)
                    km = km3.reshape(s // MOBA_BLOCK, 2 * d)
                    km = jnp.pad(km, ((0, nbp - s // MOBA_BLOCK), (0, 0)))
                qz = _proj(xb, cast(w_in_b[i - n_a]), rope=rope, rope_cols=d)
                a = _moba_attention(qz, kv, km)
                w_out = w_out_b[i - n_a]
            xb = _post(a, xb, p[i, bi], cast(w_out), ln_g[i], ln_b[i], cast(w_ple[i]),
                       cast(w_ple_gate[i]), alpha=alpha)
        outs.append(xb)
    return jnp.stack(outs, axis=0)
```

```python
import functools
import math

import jax
import jax.numpy as jnp
from jax import lax
from jax.experimental import pallas as pl
from jax.experimental.pallas import tpu as pltpu

N_HEADS = 16
HEAD_DIM = 128
ROPE_THETA = 10000.0
MOBA_BLOCK = 256
MOBA_TOPK = 3
LN_EPS = 1e-5
NEG_BIG = -1e30

LANES = 128
SUBLANES = 8
MXU_DTYPE = jnp.bfloat16
VMEM_LIMIT_BYTES = 60 * 1024 * 1024

SB_SKIP_LOG = 104.0

_NT = (((1,), (1,)), ((), ()))


def _layer_norm(x, g, b):
    mu = jnp.mean(x, axis=-1, keepdims=True)
    xc = x - mu
    var = jnp.mean(xc * xc, axis=-1, keepdims=True)
    return xc * lax.rsqrt(var + LN_EPS) * g + b


def _silu(z):
    return z * jax.nn.sigmoid(z)


def _proj_kernel(*refs, ln, rope_tiles, with_kmean, tm, tn):
    refs = list(refs)
    x_ref = refs.pop(0)
    if ln:
        g_ref = refs.pop(0)
        b_ref = refs.pop(0)
    if rope_tiles:
        cos_ref = refs.pop(0)
        sin_ref = refs.pop(0)
    w_ref = refs.pop(0)
    o_ref = refs.pop(0)
    if with_kmean:
        km_ref = refs.pop(0)
    a_scr = refs.pop(0)

    n = pl.program_id(1)

    @pl.when(n == 0)
    def _():
        xv = x_ref[...]
        if ln:
            xv = _layer_norm(xv, g_ref[...], b_ref[...])
        a_scr[...] = xv.astype(a_scr.dtype)

    acc = jnp.dot(a_scr[...], w_ref[...], preferred_element_type=jnp.float32)

    def write_plain():
        o_ref[...] = acc.astype(o_ref.dtype)
        if with_kmean:
            km_ref[...] = jnp.zeros(km_ref.shape, km_ref.dtype)

    def write_rope():
        cos2 = cos_ref[...]
        sin2 = sin_ref[...]
        for j in range(tn // HEAD_DIM):
            t = acc[:, j * HEAD_DIM:(j + 1) * HEAD_DIM]
            r = t * cos2 + pltpu.roll(t, HEAD_DIM // 2, axis=1) * sin2
            o_ref[:, j * HEAD_DIM:(j + 1) * HEAD_DIM] = r.astype(o_ref.dtype)
            if with_kmean:
                for blk in range(tm // MOBA_BLOCK):
                    rows = r[blk * MOBA_BLOCK:(blk + 1) * MOBA_BLOCK, :]
                    km_ref[blk, :, j * HEAD_DIM:(j + 1) * HEAD_DIM] = (
                        jnp.sum(rows, axis=0, keepdims=True) * (1.0 / MOBA_BLOCK))

    if rope_tiles:
        pl.when(n < rope_tiles)(write_rope)
        pl.when(n >= rope_tiles)(write_plain)
    else:
        write_plain()


def _proj(x, w, *, ln=None, rope=None, rope_cols=0, with_kmean=False, tm=1024, tn=512):
    s, d = x.shape
    n_out = w.shape[1]
    tm = min(tm, s)
    assert s % tm == 0 and n_out % tn == 0 and tn % HEAD_DIM == 0 and rope_cols % tn == 0
    assert tm % MOBA_BLOCK == 0
    rope_tiles = rope_cols // tn
    in_specs = [pl.BlockSpec((tm, d), lambda m, n: (m, 0))]
    args = [x]
    if ln is not None:
        in_specs += [pl.BlockSpec((1, d), lambda m, n: (0, 0))] * 2
        args += [ln[0].reshape(1, d), ln[1].reshape(1, d)]
    if rope_tiles:
        in_specs += [pl.BlockSpec((tm, HEAD_DIM), lambda m, n: (m, 0))] * 2
        args += [rope[0], rope[1]]
    in_specs.append(pl.BlockSpec((d, tn), lambda m, n: (0, n)))
    args.append(w)
    out_shape = [jax.ShapeDtypeStruct((s, n_out), MXU_DTYPE)]
    out_specs = [pl.BlockSpec((tm, tn), lambda m, n: (m, n))]
    if with_kmean:
        out_shape.append(jax.ShapeDtypeStruct((s // MOBA_BLOCK, 1, n_out), jnp.float32))
        out_specs.append(pl.BlockSpec((tm // MOBA_BLOCK, 1, tn), lambda m, n: (m, 0, n)))
    res = pl.pallas_call(
        functools.partial(_proj_kernel, ln=ln is not None, rope_tiles=rope_tiles,
                          with_kmean=with_kmean, tm=tm, tn=tn),
        grid=(s // tm, n_out // tn),
        in_specs=in_specs,
        out_specs=out_specs,
        out_shape=out_shape,
        scratch_shapes=[pltpu.VMEM((tm, d), MXU_DTYPE)],
        compiler_params=pltpu.CompilerParams(
            dimension_semantics=("arbitrary", "arbitrary"),
            vmem_limit_bytes=VMEM_LIMIT_BYTES),
        name="proj",
    )(*args)
    return res if with_kmean else res[0]


def _sb_kernel(q_ref, k_ref, v_ref, z_ref, o_ref, *, t, scale):
    i = pl.program_id(1)
    q = q_ref[...]
    row = lax.broadcasted_iota(jnp.int32, (t, t), 0)
    col = lax.broadcasted_iota(jnp.int32, (t, t), 1)
    tri = (row > col).astype(MXU_DTYPE)
    past = col < row

    def tile(kb, c, acc, diag):
        start = pl.multiple_of(kb * t, t)
        ks = k_ref[pl.ds(start, t), :]
        vs = v_ref[pl.ds(start, t), :]
        s = lax.dot_general(q, ks, _NT, preferred_element_type=jnp.float32) * scale
        sp = jnp.log1p(jnp.exp(-jnp.abs(s)))
        log_beta = jnp.minimum(s, 0.0) - sp
        l1m = -jnp.maximum(s, 0.0) - sp
        if diag:
            l1m = jnp.where(past, l1m, 0.0)
        hi = l1m.astype(MXU_DTYPE)
        lo = (l1m - hi.astype(jnp.float32)).astype(MXU_DTYPE)
        tail = (jnp.dot(hi, tri, preferred_element_type=jnp.float32)
                + jnp.dot(lo, tri, preferred_element_type=jnp.float32) + c)
        w = jnp.exp(log_beta + tail)
        if diag:
            w = jnp.where(past, w, 0.0)
        acc = acc + jnp.dot(w.astype(MXU_DTYPE), vs, preferred_element_type=jnp.float32)
        c = c + jnp.sum(l1m, axis=1, keepdims=True)
        return c, acc

    c0 = jnp.zeros((t, 1), jnp.float32)
    acc0 = jnp.zeros((t, HEAD_DIM), jnp.float32)
    c, acc = tile(i, c0, acc0, True)

    def cond(carry):
        kb, c, _ = carry
        return jnp.logical_and(kb >= 0, jnp.max(c) > -SB_SKIP_LOG)

    def body(carry):
        kb, c, acc = carry
        c, acc = tile(kb, c, acc, False)
        return kb - 1, c, acc

    _, _, acc = lax.while_loop(cond, body, (i - 1, c, acc))
    o_ref[...] = (acc * _silu(z_ref[...].astype(jnp.float32))).astype(o_ref.dtype)


def _sb_attention(qkvz, *, t=256):
    s = qkvz.shape[0]
    d = N_HEADS * HEAD_DIM
    assert s % t == 0
    h_ = N_HEADS
    return pl.pallas_call(
        functools.partial(_sb_kernel, t=t, scale=1.0 / math.sqrt(HEAD_DIM)),
        grid=(h_, s // t),
        in_specs=[
            pl.BlockSpec((t, HEAD_DIM), lambda h, i: (i, h)),
            pl.BlockSpec((s, HEAD_DIM), lambda h, i: (0, h_ + h)),
            pl.BlockSpec((s, HEAD_DIM), lambda h, i: (0, 2 * h_ + h)),
            pl.BlockSpec((t, HEAD_DIM), lambda h, i: (i, 3 * h_ + h)),
        ],
        out_specs=pl.BlockSpec((t, HEAD_DIM), lambda h, i: (i, h)),
        out_shape=jax.ShapeDtypeStruct((s, d), MXU_DTYPE),
        compiler_params=pltpu.CompilerParams(
            dimension_semantics=("arbitrary", "arbitrary"),
            vmem_limit_bytes=VMEM_LIMIT_BYTES),
        name="sb_attention",
    )(qkvz, qkvz, qkvz, qkvz)


MASK_BIG = 2.0 ** 100
EXP2_SAFE_HI = 100.0
EXP2_SAFE_LO = -60.0


def _moba_kernel(q_ref, z_ref, k_ref, v_ref, km_ref, o_ref, kaug, vt, *, cb, c_exp):
    t = MOBA_BLOCK
    i = pl.program_id(1)
    nbp = km_ref.shape[0]
    nb = vt.shape[0]
    f32 = jnp.float32

    @pl.when(i == 0)
    def _():
        lane = lax.broadcasted_iota(jnp.int32, (t, LANES), 1)

        def fill(b, carry):
            r0 = pl.multiple_of(b * t, t)
            kaug[pl.ds(r0, t), 0:HEAD_DIM] = k_ref[pl.ds(r0, t), :]
            kaug[pl.ds(r0, t), HEAD_DIM:HEAD_DIM + LANES] = (lane == b).astype(kaug.dtype)
            vt[b] = v_ref[pl.ds(r0, t), :].astype(f32).T.astype(vt.dtype)
            return carry

        lax.fori_loop(0, nb, fill, 0)

    qt = q_ref[...].astype(f32).T.astype(MXU_DTYPE)

    gate = jnp.dot(km_ref[...].astype(MXU_DTYPE), qt, preferred_element_type=f32)
    blk = lax.broadcasted_iota(jnp.int32, (nbp, t), 0)
    valid = blk < i
    g = jnp.where(valid, gate, NEG_BIG)
    sel = jnp.zeros((nbp, t), f32)
    for _ in range(MOBA_TOPK):
        mx = jnp.max(g, axis=0, keepdims=True)
        first = jnp.min(jnp.where(g == mx, blk, nbp), axis=0, keepdims=True)
        pick = blk == first
        sel = jnp.where(pick, 1.0, sel)
        g = jnp.where(pick, -jnp.inf, g)
    bias = jnp.where(jnp.logical_and(valid, sel > 0.0), 0.0, -MASK_BIG)
    w = jnp.concatenate([qt, bias.astype(MXU_DTYPE)], axis=0)

    key = lax.broadcasted_iota(jnp.int32, (t, t), 0)
    qry = lax.broadcasted_iota(jnp.int32, (t, t), 1)
    causal = key <= qry
    own = pl.multiple_of(i * t, t)

    def fold(x, op):
        return op(x.reshape(t // SUBLANES, SUBLANES, t), axis=0)

    def attend(shift):
        def numer(s):
            if shift is not None:
                s = s - shift
            return jnp.exp2(s * c_exp)

        s = jnp.dot(k_ref[pl.ds(own, t), :], qt, preferred_element_type=f32)
        s = jnp.where(causal, s, -MASK_BIG)
        mx = fold(s, jnp.max)
        p = numer(s)
        l = fold(p, jnp.sum)
        acc = jnp.dot(vt[i], p.astype(MXU_DTYPE), preferred_element_type=f32)

        def chunk(j, carry):
            mx, l, acc = carry
            for b in range(cb):
                kb = j * cb + b
                r0 = pl.multiple_of(kb * t, t)
                s = jnp.dot(kaug[pl.ds(r0, t), :], w, preferred_element_type=f32)
                mx = jnp.maximum(mx, fold(s, jnp.max))
                p = numer(s)
                l = l + fold(p, jnp.sum)
                acc = acc + jnp.dot(vt[kb], p.astype(MXU_DTYPE), preferred_element_type=f32)
            return mx, l, acc

        return lax.fori_loop(0, (i + cb - 1) // cb, chunk, (mx, l, acc))

    mx, l, acc = attend(None)
    top = jnp.max(mx, axis=0, keepdims=True)
    e = top * c_exp
    safe = jnp.logical_and(jnp.max(e) <= EXP2_SAFE_HI, jnp.min(e) >= EXP2_SAFE_LO)
    l, acc = lax.cond(safe, lambda: (l, acc), lambda: attend(top)[1:])
    o = (acc / jnp.sum(l, axis=0, keepdims=True)).T
    o_ref[...] = (o * _silu(z_ref[...].astype(f32))).astype(o_ref.dtype)


def _moba_attention(qz, kv, km, *, cb=8):
    s = qz.shape[0]
    d = N_HEADS * HEAD_DIM
    t = MOBA_BLOCK
    h_ = N_HEADS
    nbp = km.shape[0]
    nb = s // t
    cb = min(cb, nb)
    assert nb % cb == 0 and nbp == LANES
    c_exp = (1.0 / math.sqrt(HEAD_DIM)) * math.log2(math.e)
    return pl.pallas_call(
        functools.partial(_moba_kernel, cb=cb, c_exp=c_exp),
        grid=(h_, nb),
        in_specs=[
            pl.BlockSpec((t, HEAD_DIM), lambda h, i: (i, h)),
            pl.BlockSpec((t, HEAD_DIM), lambda h, i: (i, h_ + h)),
            pl.BlockSpec((s, HEAD_DIM), lambda h, i: (0, h)),
            pl.BlockSpec((s, HEAD_DIM), lambda h, i: (0, h_ + h)),
            pl.BlockSpec((nbp, HEAD_DIM), lambda h, i: (0, h)),
        ],
        out_specs=pl.BlockSpec((t, HEAD_DIM), lambda h, i: (i, h)),
        out_shape=jax.ShapeDtypeStruct((s, d), MXU_DTYPE),
        scratch_shapes=[pltpu.VMEM((s, HEAD_DIM + LANES), MXU_DTYPE),
                        pltpu.VMEM((nb, HEAD_DIM, t), MXU_DTYPE)],
        compiler_params=pltpu.CompilerParams(
            dimension_semantics=("arbitrary", "arbitrary"),
            vmem_limit_bytes=VMEM_LIMIT_BYTES),
        name="moba_attention",
    )(qz, qz, kv, kv, km)


def _post_kernel(a_ref, x_ref, p_ref, wo_ref, g_ref, b_ref, wp_ref, wg_ref, o_ref, *, alpha):
    y = jnp.dot(a_ref[...], wo_ref[...], preferred_element_type=jnp.float32)
    xn = _layer_norm(alpha * x_ref[...] + y, g_ref[...], b_ref[...])
    gate = jax.nn.sigmoid(jnp.dot(xn.astype(MXU_DTYPE), wg_ref[...],
                                  preferred_element_type=jnp.float32))
    e = jnp.dot(p_ref[...].astype(MXU_DTYPE), wp_ref[...], preferred_element_type=jnp.float32)
    o_ref[...] = xn + e * gate


def _post(a, x, p, w_out, g, b, w_ple, w_gate, *, alpha, tm=256):
    s, d = x.shape
    pd = p.shape[1]
    tm = min(tm, s)
    assert s % tm == 0

    def const(shape):
        return pl.BlockSpec(shape, lambda m: (0, 0), pipeline_mode=pl.Buffered(1))

    return pl.pallas_call(
        functools.partial(_post_kernel, alpha=alpha),
        grid=(s // tm,),
        in_specs=[
            pl.BlockSpec((tm, d), lambda m: (m, 0)),
            pl.BlockSpec((tm, d), lambda m: (m, 0)),
            pl.BlockSpec((tm, pd), lambda m: (m, 0)),
            const((d, d)),
            const((1, d)),
            const((1, d)),
            const((pd, d)),
            const((d, d)),
        ],
        out_specs=pl.BlockSpec((tm, d), lambda m: (m, 0)),
        out_shape=jax.ShapeDtypeStruct((s, d), jnp.float32),
        compiler_params=pltpu.CompilerParams(
            dimension_semantics=("arbitrary",),
            vmem_limit_bytes=VMEM_LIMIT_BYTES),
        name="post",
    )(a, x, p, w_out, g.reshape(1, d), b.reshape(1, d), w_ple, w_gate)


def _rope_tables(s):
    half = HEAD_DIM // 2
    inv_freq = ROPE_THETA ** (-jnp.arange(half, dtype=jnp.float32) / half)
    ang = jnp.arange(s, dtype=jnp.int32).astype(jnp.float32)[:, None] * inv_freq[None, :]
    cos = jnp.cos(ang)
    sin = jnp.sin(ang)
    return jnp.concatenate([cos, cos], axis=1), jnp.concatenate([-sin, sin], axis=1)


def kernel(x, p, w_in_a, w_out_a, w_kv, ln_kv_g, ln_kv_b, w_in_b, w_out_b, ln_g, ln_b,
           w_ple, w_ple_gate):
    bsz, s, d = x.shape
    depth = p.shape[0]
    n_a = w_in_a.shape[0]
    assert d == N_HEADS * HEAD_DIM and s % MOBA_BLOCK == 0
    alpha = (2.0 * depth) ** 0.25
    cast = lambda w: w.astype(MXU_DTYPE)
    rope = _rope_tables(s)
    nbp = -(-(s // MOBA_BLOCK) // LANES) * LANES

    outs = []
    for bi in range(bsz):
        xb = x[bi]
        kv = km = None
        for i in range(depth):
            if i < n_a:
                qkvz = _proj(xb, cast(w_in_a[i]))
                a = _sb_attention(qkvz)
                w_out = w_out_a[i]
            else:
                if i == n_a:
                    kv, km3 = _proj(xb, cast(w_kv), ln=(ln_kv_g, ln_kv_b), rope=rope,
                                    rope_cols=d, with_kmean=True)
                    km = km3.reshape(s // MOBA_BLOCK, 2 * d)
                    km = jnp.pad(km, ((0, nbp - s // MOBA_BLOCK), (0, 0)))
                qz = _proj(xb, cast(w_in_b[i - n_a]), rope=rope, rope_cols=d)
                a = _moba_attention(qz, kv, km)
                w_out = w_out_b[i - n_a]
            xb = _post(a, xb, p[i, bi], cast(w_out), ln_g[i], ln_b[i], cast(w_ple[i]),
                       cast(w_ple_gate[i]), alpha=alpha)
        outs.append(xb)
    return jnp.stack(outs, axis=0)
```

```python
import functools
import math

import jax
import jax.numpy as jnp
from jax import lax
from jax.experimental import pallas as pl
from jax.experimental.pallas import tpu as pltpu

N_HEADS = 16
HEAD_DIM = 128
ROPE_THETA = 10000.0
MOBA_BLOCK = 256
MOBA_TOPK = 3
LN_EPS = 1e-5
NEG_BIG = -1e30

LANES = 128
SUBLANES = 8
MXU_DTYPE = jnp.bfloat16
VMEM_LIMIT_BYTES = 60 * 1024 * 1024

SB_SKIP_LOG = 104.0

_NT = (((1,), (1,)), ((), ()))


def _layer_norm(x, g, b):
    mu = jnp.mean(x, axis=-1, keepdims=True)
    xc = x - mu
    var = jnp.mean(xc * xc, axis=-1, keepdims=True)
    return xc * lax.rsqrt(var + LN_EPS) * g + b


def _silu(z):
    return z * jax.nn.sigmoid(z)


def _proj_kernel(*refs, ln, rope_tiles, with_kmean, tm, tn):
    refs = list(refs)
    x_ref = refs.pop(0)
    if ln:
        g_ref = refs.pop(0)
        b_ref = refs.pop(0)
    if rope_tiles:
        cos_ref = refs.pop(0)
        sin_ref = refs.pop(0)
    w_ref = refs.pop(0)
    o_ref = refs.pop(0)
    if with_kmean:
        km_ref = refs.pop(0)
    a_scr = refs.pop(0)

    n = pl.program_id(1)

    @pl.when(n == 0)
    def _():
        xv = x_ref[...]
        if ln:
            xv = _layer_norm(xv, g_ref[...], b_ref[...])
        a_scr[...] = xv.astype(a_scr.dtype)

    acc = jnp.dot(a_scr[...], w_ref[...], preferred_element_type=jnp.float32)

    def write_plain():
        o_ref[...] = acc.astype(o_ref.dtype)
        if with_kmean:
            km_ref[...] = jnp.zeros(km_ref.shape, km_ref.dtype)

    def write_rope():
        cos2 = cos_ref[...]
        sin2 = sin_ref[...]
        for j in range(tn // HEAD_DIM):
            t = acc[:, j * HEAD_DIM:(j + 1) * HEAD_DIM]
            r = t * cos2 + pltpu.roll(t, HEAD_DIM // 2, axis=1) * sin2
            o_ref[:, j * HEAD_DIM:(j + 1) * HEAD_DIM] = r.astype(o_ref.dtype)
            if with_kmean:
                for blk in range(tm // MOBA_BLOCK):
                    rows = r[blk * MOBA_BLOCK:(blk + 1) * MOBA_BLOCK, :]
                    km_ref[blk, :, j * HEAD_DIM:(j + 1) * HEAD_DIM] = (
                        jnp.sum(rows, axis=0, keepdims=True) * (1.0 / MOBA_BLOCK))

    if rope_tiles:
        pl.when(n < rope_tiles)(write_rope)
        pl.when(n >= rope_tiles)(write_plain)
    else:
        write_plain()


def _proj(x, w, *, ln=None, rope=None, rope_cols=0, with_kmean=False, tm=1024, tn=512):
    s, d = x.shape
    n_out = w.shape[1]
    tm = min(tm, s)
    assert s % tm == 0 and n_out % tn == 0 and tn % HEAD_DIM == 0 and rope_cols % tn == 0
    assert tm % MOBA_BLOCK == 0
    rope_tiles = rope_cols // tn
    in_specs = [pl.BlockSpec((tm, d), lambda m, n: (m, 0))]
    args = [x]
    if ln is not None:
        in_specs += [pl.BlockSpec((1, d), lambda m, n: (0, 0))] * 2
        args += [ln[0].reshape(1, d), ln[1].reshape(1, d)]
    if rope_tiles:
        in_specs += [pl.BlockSpec((tm, HEAD_DIM), lambda m, n: (m, 0))] * 2
        args += [rope[0], rope[1]]
    in_specs.append(pl.BlockSpec((d, tn), lambda m, n: (0, n)))
    args.append(w)
    out_shape = [jax.ShapeDtypeStruct((s, n_out), MXU_DTYPE)]
    out_specs = [pl.BlockSpec((tm, tn), lambda m, n: (m, n))]
    if with_kmean:
        out_shape.append(jax.ShapeDtypeStruct((s // MOBA_BLOCK, 1, n_out), jnp.float32))
        out_specs.append(pl.BlockSpec((tm // MOBA_BLOCK, 1, tn), lambda m, n: (m, 0, n)))
    res = pl.pallas_call(
        functools.partial(_proj_kernel, ln=ln is not None, rope_tiles=rope_tiles,
                          with_kmean=with_kmean, tm=tm, tn=tn),
        grid=(s // tm, n_out // tn),
        in_specs=in_specs,
        out_specs=out_specs,
        out_shape=out_shape,
        scratch_shapes=[pltpu.VMEM((tm, d), MXU_DTYPE)],
        compiler_params=pltpu.CompilerParams(
            dimension_semantics=("arbitrary", "arbitrary"),
            vmem_limit_bytes=VMEM_LIMIT_BYTES),
        name="proj",
    )(*args)
    return res if with_kmean else res[0]


def _sb_kernel(q_ref, k_ref, v_ref, z_ref, o_ref, *, t, scale):
    i = pl.program_id(1)
    q = q_ref[...]
    row = lax.broadcasted_iota(jnp.int32, (t, t), 0)
    col = lax.broadcasted_iota(jnp.int32, (t, t), 1)
    tri = (row > col).astype(MXU_DTYPE)
    past = col < row

    def logits(kb, keep):
        start = pl.multiple_of(kb * t, t)
        s = lax.dot_general(q, k_ref[pl.ds(start, t), :], _NT,
                            preferred_element_type=jnp.float32) * scale
        sp = jnp.log1p(jnp.exp(-jnp.abs(s)))
        log_beta = jnp.minimum(s, 0.0) - sp
        l1m = -jnp.maximum(s, 0.0) - sp
        if keep is not None:
            l1m = jnp.where(keep, l1m, 0.0)
        hi = l1m.astype(MXU_DTYPE)
        lo = (l1m - hi.astype(jnp.float32)).astype(MXU_DTYPE)
        tail = (jnp.dot(hi, tri, preferred_element_type=jnp.float32)
                + jnp.dot(lo, tri, preferred_element_type=jnp.float32))
        return log_beta, tail, jnp.sum(l1m, axis=1, keepdims=True)

    def weighted(kb, log_beta, tail, c, keep):
        start = pl.multiple_of(kb * t, t)
        w = jnp.exp(log_beta + tail + c)
        if keep is not None:
            w = jnp.where(keep, w, 0.0)
        return jnp.dot(w.astype(MXU_DTYPE), v_ref[pl.ds(start, t), :],
                       preferred_element_type=jnp.float32)

    prev = jnp.maximum(i - 1, 0)
    has_prev = jnp.broadcast_to(i > 0, (t, t))
    lb_d, tail_d, sum_d = logits(i, past)
    lb_p, tail_p, sum_p = logits(prev, has_prev)
    acc = (weighted(i, lb_d, tail_d, 0.0, past)
           + weighted(prev, lb_p, tail_p, sum_d, has_prev))
    c = sum_d + sum_p

    def cond(carry):
        kb, c, _ = carry
        return jnp.logical_and(kb >= 0, jnp.max(c) > -SB_SKIP_LOG)

    def body(carry):
        kb, c, acc = carry
        lb, tail, tot = logits(kb, None)
        return kb - 1, c + tot, acc + weighted(kb, lb, tail, c, None)

    _, _, acc = lax.while_loop(cond, body, (i - 2, c, acc))
    o_ref[...] = (acc * _silu(z_ref[...].astype(jnp.float32))).astype(o_ref.dtype)


def _sb_attention(qkvz, *, t=256):
    s = qkvz.shape[0]
    d = N_HEADS * HEAD_DIM
    assert s % t == 0
    h_ = N_HEADS
    return pl.pallas_call(
        functools.partial(_sb_kernel, t=t, scale=1.0 / math.sqrt(HEAD_DIM)),
        grid=(h_, s // t),
        in_specs=[
            pl.BlockSpec((t, HEAD_DIM), lambda h, i: (i, h)),
            pl.BlockSpec((s, HEAD_DIM), lambda h, i: (0, h_ + h)),
            pl.BlockSpec((s, HEAD_DIM), lambda h, i: (0, 2 * h_ + h)),
            pl.BlockSpec((t, HEAD_DIM), lambda h, i: (i, 3 * h_ + h)),
        ],
        out_specs=pl.BlockSpec((t, HEAD_DIM), lambda h, i: (i, h)),
        out_shape=jax.ShapeDtypeStruct((s, d), MXU_DTYPE),
        compiler_params=pltpu.CompilerParams(
            dimension_semantics=("arbitrary", "arbitrary"),
            vmem_limit_bytes=VMEM_LIMIT_BYTES),
        name="sb_attention",
    )(qkvz, qkvz, qkvz, qkvz)


MASK_BIG = 2.0 ** 100
EXP2_SAFE_HI = 100.0
EXP2_SAFE_LO = -60.0


def _moba_kernel(q_ref, z_ref, k_ref, v_ref, km_ref, o_ref, kaug, vt, s_scr, *, cb, c_exp):
    t = MOBA_BLOCK
    i = pl.program_id(1)
    nbp = km_ref.shape[0]
    nb = vt.shape[0]
    f32 = jnp.float32

    @pl.when(i == 0)
    def _():
        lane = lax.broadcasted_iota(jnp.int32, (t, LANES), 1)

        def fill(b, carry):
            r0 = pl.multiple_of(b * t, t)
            kaug[pl.ds(r0, t), 0:HEAD_DIM] = k_ref[pl.ds(r0, t), :]
            kaug[pl.ds(r0, t), HEAD_DIM:HEAD_DIM + LANES] = (lane == b).astype(kaug.dtype)
            vt[b] = v_ref[pl.ds(r0, t), :].astype(f32).T.astype(vt.dtype)
            return carry

        lax.fori_loop(0, nb, fill, 0)

    qt = q_ref[...].astype(f32).T.astype(MXU_DTYPE)

    gate = jnp.dot(km_ref[...].astype(MXU_DTYPE), qt, preferred_element_type=f32)
    blk = lax.broadcasted_iota(jnp.int32, (nbp, t), 0)
    valid = blk < i
    g = jnp.where(valid, gate, NEG_BIG)
    sel = jnp.zeros((nbp, t), f32)
    for _ in range(MOBA_TOPK):
        mx = jnp.max(g, axis=0, keepdims=True)
        first = jnp.min(jnp.where(g == mx, blk, nbp), axis=0, keepdims=True)
        pick = blk == first
        sel = jnp.where(pick, 1.0, sel)
        g = jnp.where(pick, -jnp.inf, g)
    bias = jnp.where(jnp.logical_and(valid, sel > 0.0), 0.0, -MASK_BIG)
    w = jnp.concatenate([qt, bias.astype(MXU_DTYPE)], axis=0)

    key = lax.broadcasted_iota(jnp.int32, (t, t), 0)
    qry = lax.broadcasted_iota(jnp.int32, (t, t), 1)
    causal = key <= qry
    own = pl.multiple_of(i * t, t)

    def fold(x, op):
        return op(x.reshape(t // SUBLANES, SUBLANES, t), axis=0)

    def scores(kb):
        r0 = pl.multiple_of(kb * t, t)
        return jnp.dot(kaug[pl.ds(r0, t), :], w, preferred_element_type=f32)

    def absorb(s, kb, shift, carry):
        mx, l, acc = carry
        mx = jnp.maximum(mx, fold(s, jnp.max))
        p = jnp.exp2((s if shift is None else s - shift) * c_exp)
        l = l + fold(p, jnp.sum)
        acc = acc + jnp.dot(vt[kb], p.astype(MXU_DTYPE), preferred_element_type=f32)
        return mx, l, acc

    def own_block(shift, carry):
        s = jnp.dot(k_ref[pl.ds(own, t), :], qt, preferred_element_type=f32)
        return absorb(jnp.where(causal, s, -MASK_BIG), i, shift, carry)

    init = (jnp.full((SUBLANES, t), -MASK_BIG, f32), jnp.zeros((SUBLANES, t), f32),
            jnp.zeros((HEAD_DIM, t), f32))
    n_chunks = jnp.maximum((i + cb - 1) // cb, 1)

    def pipelined(j, carry, last):
        for b in range(cb):
            carry = absorb(s_scr[b * t:(b + 1) * t, :], j * cb + b, None, carry)
            if not last:
                s_scr[b * t:(b + 1) * t, :] = scores((j + 1) * cb + b)
        return carry

    for b in range(cb):
        s_scr[b * t:(b + 1) * t, :] = scores(b)
    carry = lax.fori_loop(0, n_chunks - 1, lambda j, c: pipelined(j, c, False), init)
    carry = pipelined(n_chunks - 1, carry, True)
    mx, l, acc = own_block(None, carry)

    top = jnp.max(mx, axis=0, keepdims=True)
    e = top * c_exp
    safe = jnp.logical_and(jnp.max(e) <= EXP2_SAFE_HI, jnp.min(e) >= EXP2_SAFE_LO)

    def shifted():
        def chunk(j, carry):
            for b in range(cb):
                carry = absorb(scores(j * cb + b), j * cb + b, top, carry)
            return carry
        return own_block(top, lax.fori_loop(0, n_chunks, chunk, init))[1:]

    l, acc = lax.cond(safe, lambda: (l, acc), shifted)
    o = (acc / jnp.sum(l, axis=0, keepdims=True)).T
    o_ref[...] = (o * _silu(z_ref[...].astype(f32))).astype(o_ref.dtype)


def _moba_attention(qz, kv, km, *, cb=8):
    s = qz.shape[0]
    d = N_HEADS * HEAD_DIM
    t = MOBA_BLOCK
    h_ = N_HEADS
    nbp = km.shape[0]
    nb = s // t
    cb = min(cb, nb)
    assert nb % cb == 0 and nbp == LANES
    c_exp = (1.0 / math.sqrt(HEAD_DIM)) * math.log2(math.e)
    return pl.pallas_call(
        functools.partial(_moba_kernel, cb=cb, c_exp=c_exp),
        grid=(h_, nb),
        in_specs=[
            pl.BlockSpec((t, HEAD_DIM), lambda h, i: (i, h)),
            pl.BlockSpec((t, HEAD_DIM), lambda h, i: (i, h_ + h)),
            pl.BlockSpec((s, HEAD_DIM), lambda h, i: (0, h)),
            pl.BlockSpec((s, HEAD_DIM), lambda h, i: (0, h_ + h)),
            pl.BlockSpec((nbp, HEAD_DIM), lambda h, i: (0, h)),
        ],
        out_specs=pl.BlockSpec((t, HEAD_DIM), lambda h, i: (i, h)),
        out_shape=jax.ShapeDtypeStruct((s, d), MXU_DTYPE),
        scratch_shapes=[pltpu.VMEM((s, HEAD_DIM + LANES), MXU_DTYPE),
                        pltpu.VMEM((nb, HEAD_DIM, t), MXU_DTYPE),
                        pltpu.VMEM((cb * t, t), jnp.float32)],
        compiler_params=pltpu.CompilerParams(
            dimension_semantics=("arbitrary", "arbitrary"),
            vmem_limit_bytes=VMEM_LIMIT_BYTES),
        name="moba_attention",
    )(qz, qz, kv, kv, km)


def _post_kernel(a_ref, x_ref, p_ref, wo_ref, g_ref, b_ref, wp_ref, wg_ref, o_ref, *, alpha):
    y = jnp.dot(a_ref[...], wo_ref[...], preferred_element_type=jnp.float32)
    xn = _layer_norm(alpha * x_ref[...] + y, g_ref[...], b_ref[...])
    gate = jax.nn.sigmoid(jnp.dot(xn.astype(MXU_DTYPE), wg_ref[...],
                                  preferred_element_type=jnp.float32))
    e = jnp.dot(p_ref[...].astype(MXU_DTYPE), wp_ref[...], preferred_element_type=jnp.float32)
    o_ref[...] = xn + e * gate


def _post(a, x, p, w_out, g, b, w_ple, w_gate, *, alpha, tm=256):
    s, d = x.shape
    pd = p.shape[1]
    tm = min(tm, s)
    assert s % tm == 0

    def const(shape):
        return pl.BlockSpec(shape, lambda m: (0, 0), pipeline_mode=pl.Buffered(1))

    return pl.pallas_call(
        functools.partial(_post_kernel, alpha=alpha),
        grid=(s // tm,),
        in_specs=[
            pl.BlockSpec((tm, d), lambda m: (m, 0)),
            pl.BlockSpec((tm, d), lambda m: (m, 0)),
            pl.BlockSpec((tm, pd), lambda m: (m, 0)),
            const((d, d)),
            const((1, d)),
            const((1, d)),
            const((pd, d)),
            const((d, d)),
        ],
        out_specs=pl.BlockSpec((tm, d), lambda m: (m, 0)),
        out_shape=jax.ShapeDtypeStruct((s, d), jnp.float32),
        compiler_params=pltpu.CompilerParams(
            dimension_semantics=("arbitrary",),
            vmem_limit_bytes=VMEM_LIMIT_BYTES),
        name="post",
    )(a, x, p, w_out, g.reshape(1, d), b.reshape(1, d), w_ple, w_gate)


def _rope_tables(s):
    half = HEAD_DIM // 2
    inv_freq = ROPE_THETA ** (-jnp.arange(half, dtype=jnp.float32) / half)
    ang = jnp.arange(s, dtype=jnp.int32).astype(jnp.float32)[:, None] * inv_freq[None, :]
    cos = jnp.cos(ang)
    sin = jnp.sin(ang)
    return jnp.concatenate([cos, cos], axis=1), jnp.concatenate([-sin, sin], axis=1)


def kernel(x, p, w_in_a, w_out_a, w_kv, ln_kv_g, ln_kv_b, w_in_b, w_out_b, ln_g, ln_b,
           w_ple, w_ple_gate):
    bsz, s, d = x.shape
    depth = p.shape[0]
    n_a = w_in_a.shape[0]
    assert d == N_HEADS * HEAD_DIM and s % MOBA_BLOCK == 0
    alpha = (2.0 * depth) ** 0.25
    cast = lambda w: w.astype(MXU_DTYPE)
    rope = _rope_tables(s)
    nbp = -(-(s // MOBA_BLOCK) // LANES) * LANES

    outs = []
    for bi in range(bsz):
        xb = x[bi]
        kv = km = None
        for i in range(depth):
            if i < n_a:
                qkvz = _proj(xb, cast(w_in_a[i]))
                a = _sb_attention(qkvz)
                w_out = w_out_a[i]
            else:
                if i == n_a:
                    kv, km3 = _proj(xb, cast(w_kv), ln=(ln_kv_g, ln_kv_b), rope=rope,
                                    rope_cols=d, with_kmean=True)
                    km = km3.reshape(s // MOBA_BLOCK, 2 * d)
                    km = jnp.pad(km, ((0, nbp - s // MOBA_BLOCK), (0, 0)))
                qz = _proj(xb, cast(w_in_b[i - n_a]), rope=rope, rope_cols=d)
                a = _moba_attention(qz, kv, km)
                w_out = w_out_b[i - n_a]
            xb = _post(a, xb, p[i, bi], cast(w_out), ln_g[i], ln_b[i], cast(w_ple[i]),
                       cast(w_ple_gate[i]), alpha=alpha)
        outs.append(xb)
    return jnp.stack(outs, axis=0)
```

```python
import functools
import math

import jax
import jax.numpy as jnp
from jax import lax
from jax.experimental import pallas as pl
from jax.experimental.pallas import tpu as pltpu

N_HEADS = 16
HEAD_DIM = 128
ROPE_THETA = 10000.0
MOBA_BLOCK = 256
MOBA_TOPK = 3
LN_EPS = 1e-5
NEG_BIG = -1e30

LANES = 128
SUBLANES = 8
MXU_DTYPE = jnp.bfloat16
VMEM_LIMIT_BYTES = 60 * 1024 * 1024

SB_SKIP_LOG = 104.0

_NT = (((1,), (1,)), ((), ()))


def _layer_norm(x, g, b):
    mu = jnp.mean(x, axis=-1, keepdims=True)
    xc = x - mu
    var = jnp.mean(xc * xc, axis=-1, keepdims=True)
    return xc * lax.rsqrt(var + LN_EPS) * g + b


def _silu(z):
    return z * jax.nn.sigmoid(z)


def _proj_kernel(*refs, ln, rope_tiles, with_kmean, tm, tn):
    refs = list(refs)
    x_ref = refs.pop(0)
    if ln:
        g_ref = refs.pop(0)
        b_ref = refs.pop(0)
    if rope_tiles:
        cos_ref = refs.pop(0)
        sin_ref = refs.pop(0)
    w_ref = refs.pop(0)
    o_ref = refs.pop(0)
    if with_kmean:
        km_ref = refs.pop(0)
    a_scr = refs.pop(0)

    n = pl.program_id(1)

    @pl.when(n == 0)
    def _():
        xv = x_ref[...]
        if ln:
            xv = _layer_norm(xv, g_ref[...], b_ref[...])
        a_scr[...] = xv.astype(a_scr.dtype)

    acc = jnp.dot(a_scr[...], w_ref[...], preferred_element_type=jnp.float32)

    def write_plain():
        o_ref[...] = acc.astype(o_ref.dtype)
        if with_kmean:
            km_ref[...] = jnp.zeros(km_ref.shape, km_ref.dtype)

    def write_rope():
        cos2 = cos_ref[...]
        sin2 = sin_ref[...]
        for j in range(tn // HEAD_DIM):
            t = acc[:, j * HEAD_DIM:(j + 1) * HEAD_DIM]
            r = t * cos2 + pltpu.roll(t, HEAD_DIM // 2, axis=1) * sin2
            o_ref[:, j * HEAD_DIM:(j + 1) * HEAD_DIM] = r.astype(o_ref.dtype)
            if with_kmean:
                for blk in range(tm // MOBA_BLOCK):
                    rows = r[blk * MOBA_BLOCK:(blk + 1) * MOBA_BLOCK, :]
                    km_ref[blk, :, j * HEAD_DIM:(j + 1) * HEAD_DIM] = (
                        jnp.sum(rows, axis=0, keepdims=True) * (1.0 / MOBA_BLOCK))

    if rope_tiles:
        pl.when(n < rope_tiles)(write_rope)
        pl.when(n >= rope_tiles)(write_plain)
    else:
        write_plain()


def _proj(x, w, *, ln=None, rope=None, rope_cols=0, with_kmean=False, tm=1024, tn=512):
    s, d = x.shape
    n_out = w.shape[1]
    tm = min(tm, s)
    assert s % tm == 0 and n_out % tn == 0 and tn % HEAD_DIM == 0 and rope_cols % tn == 0
    assert tm % MOBA_BLOCK == 0
    rope_tiles = rope_cols // tn
    in_specs = [pl.BlockSpec((tm, d), lambda m, n: (m, 0))]
    args = [x]
    if ln is not None:
        in_specs += [pl.BlockSpec((1, d), lambda m, n: (0, 0))] * 2
        args += [ln[0].reshape(1, d), ln[1].reshape(1, d)]
    if rope_tiles:
        in_specs += [pl.BlockSpec((tm, HEAD_DIM), lambda m, n: (m, 0))] * 2
        args += [rope[0], rope[1]]
    in_specs.append(pl.BlockSpec((d, tn), lambda m, n: (0, n)))
    args.append(w)
    out_shape = [jax.ShapeDtypeStruct((s, n_out), MXU_DTYPE)]
    out_specs = [pl.BlockSpec((tm, tn), lambda m, n: (m, n))]
    if with_kmean:
        out_shape.append(jax.ShapeDtypeStruct((s // MOBA_BLOCK, 1, n_out), jnp.float32))
        out_specs.append(pl.BlockSpec((tm // MOBA_BLOCK, 1, tn), lambda m, n: (m, 0, n)))
    res = pl.pallas_call(
        functools.partial(_proj_kernel, ln=ln is not None, rope_tiles=rope_tiles,
                          with_kmean=with_kmean, tm=tm, tn=tn),
        grid=(s // tm, n_out // tn),
        in_specs=in_specs,
        out_specs=out_specs,
        out_shape=out_shape,
        scratch_shapes=[pltpu.VMEM((tm, d), MXU_DTYPE)],
        compiler_params=pltpu.CompilerParams(
            dimension_semantics=("arbitrary", "arbitrary"),
            vmem_limit_bytes=VMEM_LIMIT_BYTES),
        name="proj",
    )(*args)
    return res if with_kmean else res[0]


def _sb_kernel(q_ref, k_ref, v_ref, z_ref, o_ref, *, t, scale):
    i = pl.program_id(1)
    q = q_ref[...]
    row = lax.broadcasted_iota(jnp.int32, (t, t), 0)
    col = lax.broadcasted_iota(jnp.int32, (t, t), 1)
    tri = (row > col).astype(MXU_DTYPE)
    past = col < row

    def logits(kb, keep):
        start = pl.multiple_of(kb * t, t)
        s = lax.dot_general(q, k_ref[pl.ds(start, t), :], _NT,
                            preferred_element_type=jnp.float32) * scale
        sp = jnp.log1p(jnp.exp(-jnp.abs(s)))
        log_beta = jnp.minimum(s, 0.0) - sp
        l1m = -jnp.maximum(s, 0.0) - sp
        if keep is not None:
            l1m = jnp.where(keep, l1m, 0.0)
        hi = l1m.astype(MXU_DTYPE)
        lo = (l1m - hi.astype(jnp.float32)).astype(MXU_DTYPE)
        tail = (jnp.dot(hi, tri, preferred_element_type=jnp.float32)
                + jnp.dot(lo, tri, preferred_element_type=jnp.float32))
        return log_beta, tail, jnp.sum(l1m, axis=1, keepdims=True)

    def weighted(kb, log_beta, tail, c, keep):
        start = pl.multiple_of(kb * t, t)
        w = jnp.exp(log_beta + tail + c)
        if keep is not None:
            w = jnp.where(keep, w, 0.0)
        return jnp.dot(w.astype(MXU_DTYPE), v_ref[pl.ds(start, t), :],
                       preferred_element_type=jnp.float32)

    prev = jnp.maximum(i - 1, 0)
    has_prev = jnp.broadcast_to(i > 0, (t, t))
    lb_d, tail_d, sum_d = logits(i, past)
    lb_p, tail_p, sum_p = logits(prev, has_prev)
    acc = (weighted(i, lb_d, tail_d, 0.0, past)
           + weighted(prev, lb_p, tail_p, sum_d, has_prev))
    c = sum_d + sum_p

    def cond(carry):
        kb, c, _ = carry
        return jnp.logical_and(kb >= 0, jnp.max(c) > -SB_SKIP_LOG)

    def body(carry):
        kb, c, acc = carry
        lb, tail, tot = logits(kb, None)
        return kb - 1, c + tot, acc + weighted(kb, lb, tail, c, None)

    _, _, acc = lax.while_loop(cond, body, (i - 2, c, acc))
    o_ref[...] = (acc * _silu(z_ref[...].astype(jnp.float32))).astype(o_ref.dtype)


def _sb_attention(qkvz, *, t=256):
    s = qkvz.shape[0]
    d = N_HEADS * HEAD_DIM
    assert s % t == 0
    h_ = N_HEADS
    return pl.pallas_call(
        functools.partial(_sb_kernel, t=t, scale=1.0 / math.sqrt(HEAD_DIM)),
        grid=(h_, s // t),
        in_specs=[
            pl.BlockSpec((t, HEAD_DIM), lambda h, i: (i, h)),
            pl.BlockSpec((s, HEAD_DIM), lambda h, i: (0, h_ + h)),
            pl.BlockSpec((s, HEAD_DIM), lambda h, i: (0, 2 * h_ + h)),
            pl.BlockSpec((t, HEAD_DIM), lambda h, i: (i, 3 * h_ + h)),
        ],
        out_specs=pl.BlockSpec((t, HEAD_DIM), lambda h, i: (i, h)),
        out_shape=jax.ShapeDtypeStruct((s, d), MXU_DTYPE),
        compiler_params=pltpu.CompilerParams(
            dimension_semantics=("arbitrary", "arbitrary"),
            vmem_limit_bytes=VMEM_LIMIT_BYTES),
        name="sb_attention",
    )(qkvz, qkvz, qkvz, qkvz)


MASK_BIG = 2.0 ** 100
EXP2_SAFE_HI = 100.0
EXP2_SAFE_LO = -60.0


def _moba_prep_kernel(q_ref, v_ref, km_ref, w_ref, vt_ref, *, g):
    t = MOBA_BLOCK
    f32 = jnp.float32
    nbr = km_ref.shape[0]
    kmb = km_ref[...].astype(MXU_DTYPE)
    blk = lax.broadcasted_iota(jnp.int32, (nbr, t), 0)
    for gg in range(g):
        i = pl.program_id(1) * g + gg
        rows = slice(gg * t, (gg + 1) * t)
        qt = q_ref[rows, :].astype(f32).T.astype(MXU_DTYPE)
        gate = jnp.dot(kmb, qt, preferred_element_type=f32)
        valid = blk < i
        gv = jnp.where(valid, gate, NEG_BIG)
        sel = jnp.zeros((nbr, t), f32)
        for _ in range(MOBA_TOPK):
            mx = jnp.max(gv, axis=0, keepdims=True)
            first = jnp.min(jnp.where(gv == mx, blk, nbr), axis=0, keepdims=True)
            pick = blk == first
            sel = jnp.where(pick, 1.0, sel)
            gv = jnp.where(pick, -jnp.inf, gv)
        bias = jnp.where(jnp.logical_and(valid, sel > 0.0), 0.0, -MASK_BIG)
        w_ref[gg, 0:HEAD_DIM, :] = qt
        w_ref[gg, HEAD_DIM:HEAD_DIM + nbr, :] = bias.astype(w_ref.dtype)
        if nbr < LANES:
            w_ref[gg, HEAD_DIM + nbr:, :] = jnp.zeros((LANES - nbr, t), w_ref.dtype)
        vt_ref[gg] = v_ref[rows, :].astype(f32).T.astype(vt_ref.dtype)


def _moba_prep(qz, kv, km, *, g=4):
    s = qz.shape[0]
    t = MOBA_BLOCK
    h_ = N_HEADS
    nb = s // t
    nbr = km.shape[0]
    g = min(g, nb)
    assert nb % g == 0 and nbr % SUBLANES == 0 and nb <= nbr <= LANES
    return pl.pallas_call(
        functools.partial(_moba_prep_kernel, g=g),
        grid=(h_, nb // g),
        in_specs=[
            pl.BlockSpec((g * t, HEAD_DIM), lambda h, i: (i, h)),
            pl.BlockSpec((g * t, HEAD_DIM), lambda h, i: (i, h_ + h)),
            pl.BlockSpec((nbr, HEAD_DIM), lambda h, i: (0, h)),
        ],
        out_specs=[
            pl.BlockSpec((None, g, HEAD_DIM + LANES, t), lambda h, i: (h, i, 0, 0)),
            pl.BlockSpec((None, g, HEAD_DIM, t), lambda h, i: (h, i, 0, 0)),
        ],
        out_shape=[
            jax.ShapeDtypeStruct((h_, nb, HEAD_DIM + LANES, t), MXU_DTYPE),
            jax.ShapeDtypeStruct((h_, nb, HEAD_DIM, t), MXU_DTYPE),
        ],
        compiler_params=pltpu.CompilerParams(
            dimension_semantics=("arbitrary", "arbitrary"),
            vmem_limit_bytes=VMEM_LIMIT_BYTES),
        name="moba_prep",
    )(qz, kv, km)


def _moba_kernel(w_ref, k_ref, vt_ref, z_ref, o_ref, kaug, s_scr, *, cb, c_exp):
    t = MOBA_BLOCK
    nb = vt_ref.shape[0]
    f32 = jnp.float32

    @pl.when(pl.program_id(0) == 0)
    def _():
        lane = lax.broadcasted_iota(jnp.int32, (t, LANES), 1)

        def fill(b, carry):
            r0 = pl.multiple_of(b * t, t)
            kaug[pl.ds(r0, t), HEAD_DIM:HEAD_DIM + LANES] = (lane == b).astype(kaug.dtype)
            return carry

        lax.fori_loop(0, nb, fill, 0)

    kaug[:, 0:HEAD_DIM] = k_ref[...]

    key = lax.broadcasted_iota(jnp.int32, (t, t), 0)
    qry = lax.broadcasted_iota(jnp.int32, (t, t), 1)
    causal = key <= qry

    def fold(x, op):
        return op(x.reshape(t // SUBLANES, SUBLANES, t), axis=0)

    def scores(kb, i):
        r0 = pl.multiple_of(kb * t, t)
        return jnp.dot(kaug[pl.ds(r0, t), :], w_ref[i], preferred_element_type=f32)

    def absorb(s, kb, shift, carry):
        mx, l, acc = carry
        mx = jnp.maximum(mx, fold(s, jnp.max))
        p = jnp.exp2((s if shift is None else s - shift) * c_exp)
        l = l + fold(p, jnp.sum)
        acc = acc + jnp.dot(vt_ref[kb], p.astype(MXU_DTYPE), preferred_element_type=f32)
        return mx, l, acc

    def own_block(i, shift, carry):
        own = pl.multiple_of(i * t, t)
        s = jnp.dot(k_ref[pl.ds(own, t), :], w_ref[i, 0:HEAD_DIM, :],
                    preferred_element_type=f32)
        return absorb(jnp.where(causal, s, -MASK_BIG), i, shift, carry)

    init = (jnp.full((SUBLANES, t), -MASK_BIG, f32), jnp.zeros((SUBLANES, t), f32),
            jnp.zeros((HEAD_DIM, t), f32))

    def slot(b):
        return slice(b * t, (b + 1) * t)

    for b in range(cb):
        s_scr[slot(b), :] = scores(b, 0)

    def query_block(i, _):
        n_chunks = jnp.maximum((i + cb - 1) // cb, 1)
        nxt = jnp.minimum(i + 1, nb - 1)

        def inner(j, carry):
            for b in range(cb):
                carry = absorb(s_scr[slot(b), :], j * cb + b, None, carry)
                s_scr[slot(b), :] = scores((j + 1) * cb + b, i)
            return carry

        carry = lax.fori_loop(0, n_chunks - 1, inner, init)
        for b in range(cb):
            carry = absorb(s_scr[slot(b), :], (n_chunks - 1) * cb + b, None, carry)
            s_scr[slot(b), :] = scores(b, nxt)
        mx, l, acc = own_block(i, None, carry)

        top = jnp.max(mx, axis=0, keepdims=True)
        e = top * c_exp
        safe = jnp.logical_and(jnp.max(e) <= EXP2_SAFE_HI, jnp.min(e) >= EXP2_SAFE_LO)

        def shifted():
            def chunk(j, carry):
                for b in range(cb):
                    carry = absorb(scores(j * cb + b, i), j * cb + b, top, carry)
                return carry
            return own_block(i, top, lax.fori_loop(0, n_chunks, chunk, init))[1:]

        l, acc = lax.cond(safe, lambda: (l, acc), shifted)
        o = (acc / jnp.sum(l, axis=0, keepdims=True)).T
        rows = pl.ds(pl.multiple_of(i * t, t), t)
        o_ref[rows, :] = (o * _silu(z_ref[rows, :].astype(f32))).astype(o_ref.dtype)
        return 0

    lax.fori_loop(0, nb, query_block, 0)


def _moba_attention(qz, kv, km, *, cb=8):
    s = qz.shape[0]
    d = N_HEADS * HEAD_DIM
    t = MOBA_BLOCK
    h_ = N_HEADS
    nb = s // t
    cb = min(cb, nb)
    assert nb % cb == 0
    w_all, vt = _moba_prep(qz, kv, km)
    c_exp = (1.0 / math.sqrt(HEAD_DIM)) * math.log2(math.e)

    def once(shape, index_map):
        return pl.BlockSpec(shape, index_map, pipeline_mode=pl.Buffered(1))

    return pl.pallas_call(
        functools.partial(_moba_kernel, cb=cb, c_exp=c_exp),
        grid=(h_,),
        in_specs=[
            pl.BlockSpec((None, nb, HEAD_DIM + LANES, t), lambda h: (h, 0, 0, 0)),
            once((s, HEAD_DIM), lambda h: (0, h)),
            pl.BlockSpec((None, nb, HEAD_DIM, t), lambda h: (h, 0, 0, 0)),
            once((s, HEAD_DIM), lambda h: (0, h_ + h)),
        ],
        out_specs=pl.BlockSpec((s, HEAD_DIM), lambda h: (0, h)),
        out_shape=jax.ShapeDtypeStruct((s, d), MXU_DTYPE),
        scratch_shapes=[pltpu.VMEM((s, HEAD_DIM + LANES), MXU_DTYPE),
                        pltpu.VMEM((cb * t, t), jnp.float32)],
        compiler_params=pltpu.CompilerParams(
            dimension_semantics=("arbitrary",),
            vmem_limit_bytes=VMEM_LIMIT_BYTES),
        name="moba_attention",
    )(w_all, kv, vt, qz)


def _post_kernel(a_ref, x_ref, p_ref, wo_ref, g_ref, b_ref, wp_ref, wg_ref, o_ref, *, alpha):
    y = jnp.dot(a_ref[...], wo_ref[...], preferred_element_type=jnp.float32)
    xn = _layer_norm(alpha * x_ref[...] + y, g_ref[...], b_ref[...])
    gate = jax.nn.sigmoid(jnp.dot(xn.astype(MXU_DTYPE), wg_ref[...],
                                  preferred_element_type=jnp.float32))
    e = jnp.dot(p_ref[...].astype(MXU_DTYPE), wp_ref[...], preferred_element_type=jnp.float32)
    o_ref[...] = xn + e * gate


def _post(a, x, p, w_out, g, b, w_ple, w_gate, *, alpha, tm=256):
    s, d = x.shape
    pd = p.shape[1]
    tm = min(tm, s)
    assert s % tm == 0

    def const(shape):
        return pl.BlockSpec(shape, lambda m: (0, 0), pipeline_mode=pl.Buffered(1))

    return pl.pallas_call(
        functools.partial(_post_kernel, alpha=alpha),
        grid=(s // tm,),
        in_specs=[
            pl.BlockSpec((tm, d), lambda m: (m, 0)),
            pl.BlockSpec((tm, d), lambda m: (m, 0)),
            pl.BlockSpec((tm, pd), lambda m: (m, 0)),
            const((d, d)),
            const((1, d)),
            const((1, d)),
            const((pd, d)),
            const((d, d)),
        ],
        out_specs=pl.BlockSpec((tm, d), lambda m: (m, 0)),
        out_shape=jax.ShapeDtypeStruct((s, d), jnp.float32),
        compiler_params=pltpu.CompilerParams(
            dimension_semantics=("arbitrary",),
            vmem_limit_bytes=VMEM_LIMIT_BYTES),
        name="post",
    )(a, x, p, w_out, g.reshape(1, d), b.reshape(1, d), w_ple, w_gate)


def _rope_tables(s):
    half = HEAD_DIM // 2
    inv_freq = ROPE_THETA ** (-jnp.arange(half, dtype=jnp.float32) / half)
    ang = jnp.arange(s, dtype=jnp.int32).astype(jnp.float32)[:, None] * inv_freq[None, :]
    cos = jnp.cos(ang)
    sin = jnp.sin(ang)
    return jnp.concatenate([cos, cos], axis=1), jnp.concatenate([-sin, sin], axis=1)


def kernel(x, p, w_in_a, w_out_a, w_kv, ln_kv_g, ln_kv_b, w_in_b, w_out_b, ln_g, ln_b,
           w_ple, w_ple_gate):
    bsz, s, d = x.shape
    depth = p.shape[0]
    n_a = w_in_a.shape[0]
    assert d == N_HEADS * HEAD_DIM and s % MOBA_BLOCK == 0
    alpha = (2.0 * depth) ** 0.25
    cast = lambda w: w.astype(MXU_DTYPE)
    rope = _rope_tables(s)
    nb = s // MOBA_BLOCK
    nbr = -(-nb // SUBLANES) * SUBLANES

    outs = []
    for bi in range(bsz):
        xb = x[bi]
        kv = km = None
        for i in range(depth):
            if i < n_a:
                qkvz = _proj(xb, cast(w_in_a[i]))
                a = _sb_attention(qkvz)
                w_out = w_out_a[i]
            else:
                if i == n_a:
                    kv, km3 = _proj(xb, cast(w_kv), ln=(ln_kv_g, ln_kv_b), rope=rope,
                                    rope_cols=d, with_kmean=True)
                    km = km3.reshape(nb, 2 * d)
                    km = jnp.pad(km, ((0, nbr - nb), (0, 0)))
                qz = _proj(xb, cast(w_in_b[i - n_a]), rope=rope, rope_cols=d)
                a = _moba_attention(qz, kv, km)
                w_out = w_out_b[i - n_a]
            xb = _post(a, xb, p[i, bi], cast(w_out), ln_g[i], ln_b[i], cast(w_ple[i]),
                       cast(w_ple_gate[i]), alpha=alpha)
        outs.append(xb)
    return jnp.stack(outs, axis=0)
```

```python
import functools
import math

import jax
import jax.numpy as jnp
from jax import lax
from jax.experimental import pallas as pl
from jax.experimental.pallas import tpu as pltpu

N_HEADS = 16
HEAD_DIM = 128
ROPE_THETA = 10000.0
MOBA_BLOCK = 256
MOBA_TOPK = 3
LN_EPS = 1e-5
NEG_BIG = -1e30

LANES = 128
SUBLANES = 8
MXU_DTYPE = jnp.bfloat16
VMEM_LIMIT_BYTES = 60 * 1024 * 1024

SB_SKIP_LOG = 104.0

_NT = (((1,), (1,)), ((), ()))


def _layer_norm(x, g, b):
    mu = jnp.mean(x, axis=-1, keepdims=True)
    xc = x - mu
    var = jnp.mean(xc * xc, axis=-1, keepdims=True)
    return xc * lax.rsqrt(var + LN_EPS) * g + b


def _silu(z):
    return z * jax.nn.sigmoid(z)


def _proj_kernel(*refs, ln, rope_tiles, with_kmean, tm, tn):
    refs = list(refs)
    x_ref = refs.pop(0)
    if ln:
        g_ref = refs.pop(0)
        b_ref = refs.pop(0)
    if rope_tiles:
        cos_ref = refs.pop(0)
        sin_ref = refs.pop(0)
    w_ref = refs.pop(0)
    o_ref = refs.pop(0)
    if with_kmean:
        km_ref = refs.pop(0)
    a_scr = refs.pop(0)

    n = pl.program_id(1)

    @pl.when(n == 0)
    def _():
        xv = x_ref[...]
        if ln:
            xv = _layer_norm(xv, g_ref[...], b_ref[...])
        a_scr[...] = xv.astype(a_scr.dtype)

    def matmul(cols=slice(None)):
        return jnp.dot(a_scr[...], w_ref[:, cols], preferred_element_type=jnp.float32)

    def write_plain():
        o_ref[...] = matmul().astype(o_ref.dtype)
        if with_kmean:
            km_ref[...] = jnp.zeros(km_ref.shape, km_ref.dtype)

    def write_rope():
        cos2 = cos_ref[...]
        sin2 = sin_ref[...]
        acc = matmul()
        for j in range(tn // HEAD_DIM):
            t = acc[:, j * HEAD_DIM:(j + 1) * HEAD_DIM]
            r = t * cos2 + pltpu.roll(t, HEAD_DIM // 2, axis=1) * sin2
            o_ref[:, j * HEAD_DIM:(j + 1) * HEAD_DIM] = r.astype(o_ref.dtype)
            if with_kmean:
                for blk in range(tm // MOBA_BLOCK):
                    rows = r[blk * MOBA_BLOCK:(blk + 1) * MOBA_BLOCK, :]
                    km_ref[blk, :, j * HEAD_DIM:(j + 1) * HEAD_DIM] = (
                        jnp.sum(rows, axis=0, keepdims=True) * (1.0 / MOBA_BLOCK))

    if rope_tiles:
        pl.when(n < rope_tiles)(write_rope)
        pl.when(n >= rope_tiles)(write_plain)
    else:
        write_plain()


def _proj(x, w, *, ln=None, rope=None, rope_cols=0, with_kmean=False, tm=1024, tn=1024):
    s, d = x.shape
    n_out = w.shape[1]
    tm = min(tm, s)
    assert s % tm == 0 and n_out % tn == 0 and tn % HEAD_DIM == 0 and rope_cols % tn == 0
    assert tm % MOBA_BLOCK == 0
    rope_tiles = rope_cols // tn
    in_specs = [pl.BlockSpec((tm, d), lambda m, n: (m, 0))]
    args = [x]
    if ln is not None:
        in_specs += [pl.BlockSpec((1, d), lambda m, n: (0, 0))] * 2
        args += [ln[0].reshape(1, d), ln[1].reshape(1, d)]
    if rope_tiles:
        in_specs += [pl.BlockSpec((tm, HEAD_DIM), lambda m, n: (m, 0))] * 2
        args += [rope[0], rope[1]]
    in_specs.append(pl.BlockSpec((d, tn), lambda m, n: (0, n)))
    args.append(w)
    out_shape = [jax.ShapeDtypeStruct((s, n_out), MXU_DTYPE)]
    out_specs = [pl.BlockSpec((tm, tn), lambda m, n: (m, n))]
    if with_kmean:
        out_shape.append(jax.ShapeDtypeStruct((s // MOBA_BLOCK, 1, n_out), jnp.float32))
        out_specs.append(pl.BlockSpec((tm // MOBA_BLOCK, 1, tn), lambda m, n: (m, 0, n)))
    res = pl.pallas_call(
        functools.partial(_proj_kernel, ln=ln is not None, rope_tiles=rope_tiles,
                          with_kmean=with_kmean, tm=tm, tn=tn),
        grid=(s // tm, n_out // tn),
        in_specs=in_specs,
        out_specs=out_specs,
        out_shape=out_shape,
        scratch_shapes=[pltpu.VMEM((tm, d), MXU_DTYPE)],
        compiler_params=pltpu.CompilerParams(
            dimension_semantics=("arbitrary", "arbitrary"),
            vmem_limit_bytes=VMEM_LIMIT_BYTES),
        name="proj",
    )(*args)
    return res if with_kmean else res[0]


def _sb_kernel(q_ref, k_ref, v_ref, z_ref, o_ref, *, t, scale, hp):
    i = pl.program_id(1)
    row = lax.broadcasted_iota(jnp.int32, (t, t), 0)
    col = lax.broadcasted_iota(jnp.int32, (t, t), 1)
    tri = (row > col).astype(MXU_DTYPE)
    past = col < row
    prev = jnp.maximum(i - 1, 0)
    has_prev = jnp.broadcast_to(i > 0, (t, t))
    heads = [_SbHead(q_ref, k_ref, v_ref, slice(hh * HEAD_DIM, (hh + 1) * HEAD_DIM),
                     t, scale, tri) for hh in range(hp)]
    first = [hd.first_two(i, prev, past, has_prev) for hd in heads]
    for hd, (c, acc) in zip(heads, first):
        acc = hd.walk(i - 2, c, acc)
        z = z_ref[:, hd.cols].astype(jnp.float32)
        o_ref[:, hd.cols] = (acc * _silu(z)).astype(o_ref.dtype)


class _SbHead:
    def __init__(self, q_ref, k_ref, v_ref, cols, t, scale, tri):
        self.q = q_ref[:, cols]
        self.k_ref, self.v_ref, self.cols = k_ref, v_ref, cols
        self.t, self.scale, self.tri = t, scale, tri

    def logits(self, kb, keep):
        t, tri = self.t, self.tri
        start = pl.multiple_of(kb * t, t)
        s = lax.dot_general(self.q, self.k_ref[pl.ds(start, t), self.cols], _NT,
                            preferred_element_type=jnp.float32) * self.scale
        sp = jnp.log(1.0 + jnp.exp(-jnp.abs(s)))
        log_beta = jnp.minimum(s, 0.0) - sp
        l1m = -jnp.maximum(s, 0.0) - sp
        if keep is not None:
            l1m = jnp.where(keep, l1m, 0.0)
        hi = l1m.astype(MXU_DTYPE)
        lo = (l1m - hi.astype(jnp.float32)).astype(MXU_DTYPE)
        tail = (jnp.dot(hi, tri, preferred_element_type=jnp.float32)
                + jnp.dot(lo, tri, preferred_element_type=jnp.float32))
        return log_beta, tail, jnp.sum(l1m, axis=1, keepdims=True)

    def weighted(self, kb, log_beta, tail, c, keep):
        t = self.t
        start = pl.multiple_of(kb * t, t)
        w = jnp.exp(log_beta + tail + c)
        if keep is not None:
            w = jnp.where(keep, w, 0.0)
        return jnp.dot(w.astype(MXU_DTYPE), self.v_ref[pl.ds(start, t), self.cols],
                       preferred_element_type=jnp.float32)

    def first_two(self, i, prev, past, has_prev):
        lb_d, tail_d, sum_d = self.logits(i, past)
        lb_p, tail_p, sum_p = self.logits(prev, has_prev)
        acc = (self.weighted(i, lb_d, tail_d, 0.0, past)
               + self.weighted(prev, lb_p, tail_p, sum_d, has_prev))
        return sum_d + sum_p, acc

    def walk(self, kb0, c, acc):
        def cond(carry):
            kb, c, _ = carry
            return jnp.logical_and(kb >= 0, jnp.max(c) > -SB_SKIP_LOG)

        def body(carry):
            kb, c, acc = carry
            lb, tail, tot = self.logits(kb, None)
            return kb - 1, c + tot, acc + self.weighted(kb, lb, tail, c, None)

        return lax.while_loop(cond, body, (kb0, c, acc))[2]


def _sb_attention(qkvz, *, t=256, hp=2):
    s = qkvz.shape[0]
    d = N_HEADS * HEAD_DIM
    assert s % t == 0 and N_HEADS % hp == 0
    h_ = N_HEADS // hp
    wd = hp * HEAD_DIM
    return pl.pallas_call(
        functools.partial(_sb_kernel, t=t, scale=1.0 / math.sqrt(HEAD_DIM), hp=hp),
        grid=(h_, s // t),
        in_specs=[
            pl.BlockSpec((t, wd), lambda h, i: (i, h)),
            pl.BlockSpec((s, wd), lambda h, i: (0, h_ + h)),
            pl.BlockSpec((s, wd), lambda h, i: (0, 2 * h_ + h)),
            pl.BlockSpec((t, wd), lambda h, i: (i, 3 * h_ + h)),
        ],
        out_specs=pl.BlockSpec((t, wd), lambda h, i: (i, h)),
        out_shape=jax.ShapeDtypeStruct((s, d), MXU_DTYPE),
        compiler_params=pltpu.CompilerParams(
            dimension_semantics=("arbitrary", "arbitrary"),
            vmem_limit_bytes=VMEM_LIMIT_BYTES),
        name="sb_attention",
    )(qkvz, qkvz, qkvz, qkvz)


MASK_BIG = 2.0 ** 100
EXP2_SAFE_HI = 100.0
EXP2_SAFE_LO = -60.0


def _moba_prep_kernel(q_ref, v_ref, km_ref, w_ref, vt_ref, *, g):
    t = MOBA_BLOCK
    f32 = jnp.float32
    nbr = km_ref.shape[0]
    kmb = km_ref[...].astype(MXU_DTYPE)
    blk = lax.broadcasted_iota(jnp.int32, (nbr, t), 0)
    for gg in range(g):
        i = pl.program_id(1) * g + gg
        rows = slice(gg * t, (gg + 1) * t)
        qt = q_ref[rows, :].astype(f32).T.astype(MXU_DTYPE)
        gate = jnp.dot(kmb, qt, preferred_element_type=f32)
        valid = blk < i
        gv = jnp.where(valid, gate, NEG_BIG)
        sel = jnp.zeros((nbr, t), f32)
        for _ in range(MOBA_TOPK):
            mx = jnp.max(gv, axis=0, keepdims=True)
            first = jnp.min(jnp.where(gv == mx, blk, nbr), axis=0, keepdims=True)
            pick = blk == first
            sel = jnp.where(pick, 1.0, sel)
            gv = jnp.where(pick, -jnp.inf, gv)
        bias = jnp.where(jnp.logical_and(valid, sel > 0.0), 0.0, -MASK_BIG)
        w_ref[gg, 0:HEAD_DIM, :] = qt
        w_ref[gg, HEAD_DIM:HEAD_DIM + nbr, :] = bias.astype(w_ref.dtype)
        if nbr < LANES:
            w_ref[gg, HEAD_DIM + nbr:, :] = jnp.zeros((LANES - nbr, t), w_ref.dtype)
        vt_ref[gg] = v_ref[rows, :].astype(f32).T.astype(vt_ref.dtype)


def _moba_prep(qz, kv, km, *, g=4):
    s = qz.shape[0]
    t = MOBA_BLOCK
    h_ = N_HEADS
    nb = s // t
    nbr = km.shape[0]
    g = min(g, nb)
    assert nb % g == 0 and nbr % SUBLANES == 0 and nb <= nbr <= LANES
    return pl.pallas_call(
        functools.partial(_moba_prep_kernel, g=g),
        grid=(h_, nb // g),
        in_specs=[
            pl.BlockSpec((g * t, HEAD_DIM), lambda h, i: (i, h)),
            pl.BlockSpec((g * t, HEAD_DIM), lambda h, i: (i, h_ + h)),
            pl.BlockSpec((nbr, HEAD_DIM), lambda h, i: (0, h)),
        ],
        out_specs=[
            pl.BlockSpec((None, g, HEAD_DIM + LANES, t), lambda h, i: (h, i, 0, 0)),
            pl.BlockSpec((None, g, HEAD_DIM, t), lambda h, i: (h, i, 0, 0)),
        ],
        out_shape=[
            jax.ShapeDtypeStruct((h_, nb, HEAD_DIM + LANES, t), MXU_DTYPE),
            jax.ShapeDtypeStruct((h_, nb, HEAD_DIM, t), MXU_DTYPE),
        ],
        compiler_params=pltpu.CompilerParams(
            dimension_semantics=("arbitrary", "arbitrary"),
            vmem_limit_bytes=VMEM_LIMIT_BYTES),
        name="moba_prep",
    )(qz, kv, km)


def _moba_kernel(w_ref, k_ref, vt_ref, z_ref, o_ref, kaug, s_scr, *, cb, c_exp):
    t = MOBA_BLOCK
    nb = vt_ref.shape[0]
    f32 = jnp.float32

    @pl.when(pl.program_id(0) == 0)
    def _():
        lane = lax.broadcasted_iota(jnp.int32, (t, LANES), 1)

        def fill(b, carry):
            r0 = pl.multiple_of(b * t, t)
            kaug[pl.ds(r0, t), HEAD_DIM:HEAD_DIM + LANES] = (lane == b).astype(kaug.dtype)
            return carry

        lax.fori_loop(0, nb, fill, 0)

    kaug[:, 0:HEAD_DIM] = k_ref[...]

    key = lax.broadcasted_iota(jnp.int32, (t, t), 0)
    qry = lax.broadcasted_iota(jnp.int32, (t, t), 1)
    causal = key <= qry

    def fold(x, op):
        return op(x.reshape(t // SUBLANES, SUBLANES, t), axis=0)

    def scores(kb, i):
        r0 = pl.multiple_of(kb * t, t)
        return jnp.dot(kaug[pl.ds(r0, t), :], w_ref[i], preferred_element_type=f32)

    def absorb(s, kb, shift, carry):
        mx, l, acc = carry
        mx = jnp.maximum(mx, fold(s, jnp.max))
        p = jnp.exp2((s if shift is None else s - shift) * c_exp)
        l = l + fold(p, jnp.sum)
        acc = acc + jnp.dot(vt_ref[kb], p.astype(MXU_DTYPE), preferred_element_type=f32)
        return mx, l, acc

    def own_block(i, shift, carry):
        own = pl.multiple_of(i * t, t)
        s = jnp.dot(k_ref[pl.ds(own, t), :], w_ref[i, 0:HEAD_DIM, :],
                    preferred_element_type=f32)
        return absorb(jnp.where(causal, s, -MASK_BIG), i, shift, carry)

    init = (jnp.full((SUBLANES, t), -MASK_BIG, f32), jnp.zeros((SUBLANES, t), f32),
            jnp.zeros((HEAD_DIM, t), f32))

    def slot(b):
        return slice(b * t, (b + 1) * t)

    for b in range(cb):
        s_scr[slot(b), :] = scores(b, 0)

    def query_block(i, _):
        n_chunks = jnp.maximum((i + cb - 1) // cb, 1)
        nxt = jnp.minimum(i + 1, nb - 1)

        def inner(j, carry):
            for b in range(cb):
                carry = absorb(s_scr[slot(b), :], j * cb + b, None, carry)
                s_scr[slot(b), :] = scores((j + 1) * cb + b, i)
            return carry

        carry = lax.fori_loop(0, n_chunks - 1, inner, init)
        for b in range(cb):
            carry = absorb(s_scr[slot(b), :], (n_chunks - 1) * cb + b, None, carry)
            s_scr[slot(b), :] = scores(b, nxt)
        mx, l, acc = own_block(i, None, carry)

        top = jnp.max(mx, axis=0, keepdims=True)
        e = top * c_exp
        safe = jnp.logical_and(jnp.max(e) <= EXP2_SAFE_HI, jnp.min(e) >= EXP2_SAFE_LO)

        def shifted():
            def chunk(j, carry):
                for b in range(cb):
                    carry = absorb(scores(j * cb + b, i), j * cb + b, top, carry)
                return carry
            return own_block(i, top, lax.fori_loop(0, n_chunks, chunk, init))[1:]

        l, acc = lax.cond(safe, lambda: (l, acc), shifted)
        o = (acc / jnp.sum(l, axis=0, keepdims=True)).T
        rows = pl.ds(pl.multiple_of(i * t, t), t)
        o_ref[rows, :] = (o * _silu(z_ref[rows, :].astype(f32))).astype(o_ref.dtype)
        return 0

    lax.fori_loop(0, nb, query_block, 0)


def _moba_attention(qz, kv, km, *, cb=8):
    s = qz.shape[0]
    d = N_HEADS * HEAD_DIM
    t = MOBA_BLOCK
    h_ = N_HEADS
    nb = s // t
    cb = min(cb, nb)
    assert nb % cb == 0
    w_all, vt = _moba_prep(qz, kv, km)
    c_exp = (1.0 / math.sqrt(HEAD_DIM)) * math.log2(math.e)

    def once(shape, index_map):
        return pl.BlockSpec(shape, index_map, pipeline_mode=pl.Buffered(1))

    return pl.pallas_call(
        functools.partial(_moba_kernel, cb=cb, c_exp=c_exp),
        grid=(h_,),
        in_specs=[
            pl.BlockSpec((None, nb, HEAD_DIM + LANES, t), lambda h: (h, 0, 0, 0)),
            once((s, HEAD_DIM), lambda h: (0, h)),
            pl.BlockSpec((None, nb, HEAD_DIM, t), lambda h: (h, 0, 0, 0)),
            once((s, HEAD_DIM), lambda h: (0, h_ + h)),
        ],
        out_specs=pl.BlockSpec((s, HEAD_DIM), lambda h: (0, h)),
        out_shape=jax.ShapeDtypeStruct((s, d), MXU_DTYPE),
        scratch_shapes=[pltpu.VMEM((s, HEAD_DIM + LANES), MXU_DTYPE),
                        pltpu.VMEM((cb * t, t), jnp.float32)],
        compiler_params=pltpu.CompilerParams(
            dimension_semantics=("arbitrary",),
            vmem_limit_bytes=VMEM_LIMIT_BYTES),
        name="moba_attention",
    )(w_all, kv, vt, qz)


def _post_kernel(a_ref, x_ref, p_ref, wo_ref, g_ref, b_ref, wp_ref, wg_ref, o_ref, *, alpha):
    y = jnp.dot(a_ref[...], wo_ref[...], preferred_element_type=jnp.float32)
    xn = _layer_norm(alpha * x_ref[...] + y, g_ref[...], b_ref[...])
    gate = jax.nn.sigmoid(jnp.dot(xn.astype(MXU_DTYPE), wg_ref[...],
                                  preferred_element_type=jnp.float32))
    e = jnp.dot(p_ref[...].astype(MXU_DTYPE), wp_ref[...], preferred_element_type=jnp.float32)
    o_ref[...] = xn + e * gate


def _post(a, x, p, w_out, g, b, w_ple, w_gate, *, alpha, tm=256):
    s, d = x.shape
    pd = p.shape[1]
    tm = min(tm, s)
    assert s % tm == 0

    def const(shape):
        return pl.BlockSpec(shape, lambda m: (0, 0), pipeline_mode=pl.Buffered(1))

    return pl.pallas_call(
        functools.partial(_post_kernel, alpha=alpha),
        grid=(s // tm,),
        in_specs=[
            pl.BlockSpec((tm, d), lambda m: (m, 0)),
            pl.BlockSpec((tm, d), lambda m: (m, 0)),
            pl.BlockSpec((tm, pd), lambda m: (m, 0)),
            const((d, d)),
            const((1, d)),
            const((1, d)),
            const((pd, d)),
            const((d, d)),
        ],
        out_specs=pl.BlockSpec((tm, d), lambda m: (m, 0)),
        out_shape=jax.ShapeDtypeStruct((s, d), jnp.float32),
        compiler_params=pltpu.CompilerParams(
            dimension_semantics=("arbitrary",),
            vmem_limit_bytes=VMEM_LIMIT_BYTES),
        name="post",
    )(a, x, p, w_out, g.reshape(1, d), b.reshape(1, d), w_ple, w_gate)


def _rope_tables(s):
    half = HEAD_DIM // 2
    inv_freq = ROPE_THETA ** (-jnp.arange(half, dtype=jnp.float32) / half)
    ang = jnp.arange(s, dtype=jnp.int32).astype(jnp.float32)[:, None] * inv_freq[None, :]
    cos = jnp.cos(ang)
    sin = jnp.sin(ang)
    return jnp.concatenate([cos, cos], axis=1), jnp.concatenate([-sin, sin], axis=1)


def kernel(x, p, w_in_a, w_out_a, w_kv, ln_kv_g, ln_kv_b, w_in_b, w_out_b, ln_g, ln_b,
           w_ple, w_ple_gate):
    bsz, s, d = x.shape
    depth = p.shape[0]
    n_a = w_in_a.shape[0]
    assert d == N_HEADS * HEAD_DIM and s % MOBA_BLOCK == 0
    alpha = (2.0 * depth) ** 0.25
    cast = lambda w: w.astype(MXU_DTYPE)
    rope = _rope_tables(s)
    nb = s // MOBA_BLOCK
    nbr = -(-nb // SUBLANES) * SUBLANES

    outs = []
    for bi in range(bsz):
        xb = x[bi]
        kv = km = None
        for i in range(depth):
            if i < n_a:
                qkvz = _proj(xb, cast(w_in_a[i]))
                a = _sb_attention(qkvz)
                w_out = w_out_a[i]
            else:
                if i == n_a:
                    kv, km3 = _proj(xb, cast(w_kv), ln=(ln_kv_g, ln_kv_b), rope=rope,
                                    rope_cols=d, with_kmean=True)
                    km = km3.reshape(nb, 2 * d)
                    km = jnp.pad(km, ((0, nbr - nb), (0, 0)))
                qz = _proj(xb, cast(w_in_b[i - n_a]), rope=rope, rope_cols=d)
                a = _moba_attention(qz, kv, km)
                w_out = w_out_b[i - n_a]
            xb = _post(a, xb, p[i, bi], cast(w_out), ln_g[i], ln_b[i], cast(w_ple[i]),
                       cast(w_ple_gate[i]), alpha=alpha)
        outs.append(xb)
    return jnp.stack(outs, axis=0)
```

```python
import functools
import math

import jax
import jax.numpy as jnp
from jax import lax
from jax.experimental import pallas as pl
from jax.experimental.pallas import tpu as pltpu

N_HEADS = 16
HEAD_DIM = 128
ROPE_THETA = 10000.0
MOBA_BLOCK = 256
MOBA_TOPK = 3
LN_EPS = 1e-5
NEG_BIG = -1e30

LANES = 128
SUBLANES = 8
MXU_DTYPE = jnp.bfloat16
VMEM_LIMIT_BYTES = 60 * 1024 * 1024

SB_SKIP_LOG = 104.0

_NT = (((1,), (1,)), ((), ()))


def _layer_norm(x, g, b):
    mu = jnp.mean(x, axis=-1, keepdims=True)
    xc = x - mu
    var = jnp.mean(xc * xc, axis=-1, keepdims=True)
    return xc * lax.rsqrt(var + LN_EPS) * g + b


def _silu(z):
    return z * jax.nn.sigmoid(z)


def _proj_kernel(*refs, ln, rope_tiles, with_kmean, tm, tn):
    refs = list(refs)
    x_ref = refs.pop(0)
    if ln:
        g_ref = refs.pop(0)
        b_ref = refs.pop(0)
    if rope_tiles:
        cos_ref = refs.pop(0)
        sin_ref = refs.pop(0)
    w_ref = refs.pop(0)
    o_ref = refs.pop(0)
    if with_kmean:
        km_ref = refs.pop(0)
    a_scr = refs.pop(0)

    n = pl.program_id(1)

    @pl.when(n == 0)
    def _():
        xv = x_ref[...]
        if ln:
            xv = _layer_norm(xv, g_ref[...], b_ref[...])
        a_scr[...] = xv.astype(a_scr.dtype)

    def matmul(cols=slice(None)):
        return jnp.dot(a_scr[...], w_ref[:, cols], preferred_element_type=jnp.float32)

    def write_plain():
        o_ref[...] = matmul().astype(o_ref.dtype)
        if with_kmean:
            km_ref[...] = jnp.zeros(km_ref.shape, km_ref.dtype)

    def write_rope():
        cos2 = cos_ref[...]
        sin2 = sin_ref[...]
        acc = matmul()
        for j in range(tn // HEAD_DIM):
            t = acc[:, j * HEAD_DIM:(j + 1) * HEAD_DIM]
            r = t * cos2 + pltpu.roll(t, HEAD_DIM // 2, axis=1) * sin2
            o_ref[:, j * HEAD_DIM:(j + 1) * HEAD_DIM] = r.astype(o_ref.dtype)
            if with_kmean:
                for blk in range(tm // MOBA_BLOCK):
                    rows = r[blk * MOBA_BLOCK:(blk + 1) * MOBA_BLOCK, :]
                    km_ref[blk, :, j * HEAD_DIM:(j + 1) * HEAD_DIM] = (
                        jnp.sum(rows, axis=0, keepdims=True) * (1.0 / MOBA_BLOCK))

    if rope_tiles:
        pl.when(n < rope_tiles)(write_rope)
        pl.when(n >= rope_tiles)(write_plain)
    else:
        write_plain()


def _proj(x, w, *, ln=None, rope=None, rope_cols=0, with_kmean=False, tm=1024, tn=1024):
    s, d = x.shape
    n_out = w.shape[1]
    tm = min(tm, s)
    assert s % tm == 0 and n_out % tn == 0 and tn % HEAD_DIM == 0 and rope_cols % tn == 0
    assert tm % MOBA_BLOCK == 0
    rope_tiles = rope_cols // tn
    in_specs = [pl.BlockSpec((tm, d), lambda m, n: (m, 0))]
    args = [x]
    if ln is not None:
        in_specs += [pl.BlockSpec((1, d), lambda m, n: (0, 0))] * 2
        args += [ln[0].reshape(1, d), ln[1].reshape(1, d)]
    if rope_tiles:
        in_specs += [pl.BlockSpec((tm, HEAD_DIM), lambda m, n: (m, 0))] * 2
        args += [rope[0], rope[1]]
    in_specs.append(pl.BlockSpec((d, tn), lambda m, n: (0, n)))
    args.append(w)
    out_shape = [jax.ShapeDtypeStruct((s, n_out), MXU_DTYPE)]
    out_specs = [pl.BlockSpec((tm, tn), lambda m, n: (m, n))]
    if with_kmean:
        out_shape.append(jax.ShapeDtypeStruct((s // MOBA_BLOCK, 1, n_out), jnp.float32))
        out_specs.append(pl.BlockSpec((tm // MOBA_BLOCK, 1, tn), lambda m, n: (m, 0, n)))
    res = pl.pallas_call(
        functools.partial(_proj_kernel, ln=ln is not None, rope_tiles=rope_tiles,
                          with_kmean=with_kmean, tm=tm, tn=tn),
        grid=(s // tm, n_out // tn),
        in_specs=in_specs,
        out_specs=out_specs,
        out_shape=out_shape,
        scratch_shapes=[pltpu.VMEM((tm, d), MXU_DTYPE)],
        compiler_params=pltpu.CompilerParams(
            dimension_semantics=("arbitrary", "arbitrary"),
            vmem_limit_bytes=VMEM_LIMIT_BYTES),
        name="proj",
    )(*args)
    return res if with_kmean else res[0]


def _sb_kernel(q_ref, k_ref, v_ref, z_ref, o_ref, *, t, scale, hp):
    i = pl.program_id(1)
    row = lax.broadcasted_iota(jnp.int32, (t, t), 0)
    col = lax.broadcasted_iota(jnp.int32, (t, t), 1)
    tri = (row > col).astype(MXU_DTYPE)
    past = col < row
    prev = jnp.maximum(i - 1, 0)
    has_prev = i > 0
    heads = [_SbHead(q_ref, k_ref, v_ref, slice(hh * HEAD_DIM, (hh + 1) * HEAD_DIM),
                     t, scale, tri) for hh in range(hp)]
    first = [hd.first_two(i, prev, past, has_prev) for hd in heads]
    for hd, (c, acc) in zip(heads, first):
        acc = hd.walk(i - 2, c, acc)
        z = z_ref[:, hd.cols].astype(jnp.float32)
        o_ref[:, hd.cols] = (acc * _silu(z)).astype(o_ref.dtype)


class _SbHead:
    def __init__(self, q_ref, k_ref, v_ref, cols, t, scale, tri):
        self.q = q_ref[:, cols]
        self.k_ref, self.v_ref, self.cols = k_ref, v_ref, cols
        self.t, self.scale, self.tri = t, scale, tri

    def logits(self, kb, keep):
        t, tri = self.t, self.tri
        start = pl.multiple_of(kb * t, t)
        s = lax.dot_general(self.q, self.k_ref[pl.ds(start, t), self.cols], _NT,
                            preferred_element_type=jnp.float32) * self.scale
        sp = jnp.log(1.0 + jnp.exp(-jnp.abs(s)))
        log_beta = jnp.minimum(s, 0.0) - sp
        l1m = -jnp.maximum(s, 0.0) - sp
        if keep is not None:
            l1m = jnp.where(keep, l1m, 0.0)
        hi = l1m.astype(MXU_DTYPE)
        lo = (l1m - hi.astype(jnp.float32)).astype(MXU_DTYPE)
        tail = (jnp.dot(hi, tri, preferred_element_type=jnp.float32)
                + jnp.dot(lo, tri, preferred_element_type=jnp.float32))
        return log_beta, tail, jnp.sum(l1m, axis=1, keepdims=True)

    def weighted(self, kb, log_beta, tail, c, keep):
        t = self.t
        start = pl.multiple_of(kb * t, t)
        w = jnp.exp(log_beta + tail + c)
        if keep is not None:
            w = jnp.where(keep, w, 0.0)
        return jnp.dot(w.astype(MXU_DTYPE), self.v_ref[pl.ds(start, t), self.cols],
                       preferred_element_type=jnp.float32)

    def first_two(self, i, prev, past, has_prev):
        lb_d, tail_d, sum_d = self.logits(i, past)
        lb_p, tail_p, sum_p = self.logits(prev, None)
        acc_p = self.weighted(prev, lb_p, tail_p, sum_d, None)
        acc = self.weighted(i, lb_d, tail_d, 0.0, past) + jnp.where(has_prev, acc_p, 0.0)
        return sum_d + jnp.where(has_prev, sum_p, 0.0), acc

    def walk(self, kb0, c, acc):
        def cond(carry):
            kb, c, _ = carry
            return jnp.logical_and(kb >= 0, jnp.max(c) > -SB_SKIP_LOG)

        def body(carry):
            kb, c, acc = carry
            lb, tail, tot = self.logits(kb, None)
            return kb - 1, c + tot, acc + self.weighted(kb, lb, tail, c, None)

        return lax.while_loop(cond, body, (kb0, c, acc))[2]


def _sb_attention(qkvz, *, t=256, hp=2):
    s = qkvz.shape[0]
    d = N_HEADS * HEAD_DIM
    assert s % t == 0 and N_HEADS % hp == 0
    h_ = N_HEADS // hp
    wd = hp * HEAD_DIM
    return pl.pallas_call(
        functools.partial(_sb_kernel, t=t, scale=1.0 / math.sqrt(HEAD_DIM), hp=hp),
        grid=(h_, s // t),
        in_specs=[
            pl.BlockSpec((t, wd), lambda h, i: (i, h)),
            pl.BlockSpec((s, wd), lambda h, i: (0, h_ + h)),
            pl.BlockSpec((s, wd), lambda h, i: (0, 2 * h_ + h)),
            pl.BlockSpec((t, wd), lambda h, i: (i, 3 * h_ + h)),
        ],
        out_specs=pl.BlockSpec((t, wd), lambda h, i: (i, h)),
        out_shape=jax.ShapeDtypeStruct((s, d), MXU_DTYPE),
        compiler_params=pltpu.CompilerParams(
            dimension_semantics=("arbitrary", "arbitrary"),
            vmem_limit_bytes=VMEM_LIMIT_BYTES),
        name="sb_attention",
    )(qkvz, qkvz, qkvz, qkvz)


MASK_BIG = 2.0 ** 100
EXP2_SAFE_HI = 100.0
EXP2_SAFE_LO = -60.0


def _moba_prep_kernel(q_ref, v_ref, km_ref, w_ref, vt_ref, *, g):
    t = MOBA_BLOCK
    f32 = jnp.float32
    nbr = km_ref.shape[0]
    kmb = km_ref[...].astype(MXU_DTYPE)
    blk = lax.broadcasted_iota(jnp.int32, (nbr, t), 0)
    for gg in range(g):
        i = pl.program_id(1) * g + gg
        rows = slice(gg * t, (gg + 1) * t)
        qt = q_ref[rows, :].astype(f32).T.astype(MXU_DTYPE)
        gate = jnp.dot(kmb, qt, preferred_element_type=f32)
        valid = blk < i
        gv = jnp.where(valid, gate, NEG_BIG)
        sel = jnp.zeros((nbr, t), f32)
        for _ in range(MOBA_TOPK):
            mx = jnp.max(gv, axis=0, keepdims=True)
            first = jnp.min(jnp.where(gv == mx, blk, nbr), axis=0, keepdims=True)
            pick = blk == first
            sel = jnp.where(pick, 1.0, sel)
            gv = jnp.where(pick, -jnp.inf, gv)
        bias = jnp.where(jnp.logical_and(valid, sel > 0.0), 0.0, -MASK_BIG)
        w_ref[gg, 0:HEAD_DIM, :] = qt
        w_ref[gg, HEAD_DIM:HEAD_DIM + nbr, :] = bias.astype(w_ref.dtype)
        if nbr < LANES:
            w_ref[gg, HEAD_DIM + nbr:, :] = jnp.zeros((LANES - nbr, t), w_ref.dtype)
        vt_ref[gg] = v_ref[rows, :].astype(f32).T.astype(vt_ref.dtype)


def _moba_prep(qz, kv, km, *, g=4):
    s = qz.shape[0]
    t = MOBA_BLOCK
    h_ = N_HEADS
    nb = s // t
    nbr = km.shape[0]
    g = min(g, nb)
    assert nb % g == 0 and nbr % SUBLANES == 0 and nb <= nbr <= LANES
    return pl.pallas_call(
        functools.partial(_moba_prep_kernel, g=g),
        grid=(h_, nb // g),
        in_specs=[
            pl.BlockSpec((g * t, HEAD_DIM), lambda h, i: (i, h)),
            pl.BlockSpec((g * t, HEAD_DIM), lambda h, i: (i, h_ + h)),
            pl.BlockSpec((nbr, HEAD_DIM), lambda h, i: (0, h)),
        ],
        out_specs=[
            pl.BlockSpec((None, g, HEAD_DIM + LANES, t), lambda h, i: (h, i, 0, 0)),
            pl.BlockSpec((None, g, HEAD_DIM, t), lambda h, i: (h, i, 0, 0)),
        ],
        out_shape=[
            jax.ShapeDtypeStruct((h_, nb, HEAD_DIM + LANES, t), MXU_DTYPE),
            jax.ShapeDtypeStruct((h_, nb, HEAD_DIM, t), MXU_DTYPE),
        ],
        compiler_params=pltpu.CompilerParams(
            dimension_semantics=("arbitrary", "arbitrary"),
            vmem_limit_bytes=VMEM_LIMIT_BYTES),
        name="moba_prep",
    )(qz, kv, km)


def _moba_kernel(w_ref, k_ref, vt_ref, z_ref, o_ref, kaug, s_scr, *, cb, ga, c_exp):
    t = MOBA_BLOCK
    nb = vt_ref.shape[0]
    f32 = jnp.float32

    @pl.when(pl.program_id(0) == 0)
    def _():
        lane = lax.broadcasted_iota(jnp.int32, (t, LANES), 1)

        def fill(b, carry):
            r0 = pl.multiple_of(b * t, t)
            kaug[pl.ds(r0, t), HEAD_DIM:HEAD_DIM + LANES] = (lane == b).astype(kaug.dtype)
            return carry

        lax.fori_loop(0, nb, fill, 0)

    kaug[:, 0:HEAD_DIM] = k_ref[...]

    key = lax.broadcasted_iota(jnp.int32, (t, t), 0)
    qry = lax.broadcasted_iota(jnp.int32, (t, t), 1)
    causal = key <= qry

    def fold(x, op):
        return op(x.reshape(t // SUBLANES, SUBLANES, t), axis=0)

    def scores(kb, i):
        r0 = pl.multiple_of(kb * t, t)
        return jnp.dot(kaug[pl.ds(r0, t), :], w_ref[i], preferred_element_type=f32)

    def absorb(s, kb, shift, carry):
        mx, l, acc = carry
        mx = jnp.maximum(mx, fold(s, jnp.max))
        p = jnp.exp2((s if shift is None else s - shift) * c_exp)
        l = l + fold(p, jnp.sum)
        acc = acc + jnp.dot(vt_ref[kb], p.astype(MXU_DTYPE), preferred_element_type=f32)
        return mx, l, acc

    def own_block(i, shift, carry):
        own = pl.multiple_of(i * t, t)
        s = jnp.dot(k_ref[pl.ds(own, t), :], w_ref[i, 0:HEAD_DIM, :],
                    preferred_element_type=f32)
        return absorb(jnp.where(causal, s, -MASK_BIG), i, shift, carry)

    init = (jnp.full((SUBLANES, t), -MASK_BIG, f32), jnp.zeros((SUBLANES, t), f32),
            jnp.zeros((HEAD_DIM, t), f32))

    def slot(b):
        return slice(b * t, (b + 1) * t)

    for b in range(cb):
        s_scr[slot(b), :] = scores(b, 0)

    def query_block(i, _):
        n_chunks = jnp.maximum((i + cb - 1) // cb, 1)
        nxt = jnp.minimum(i + 1, nb - 1)

        def refill(b, kb0, iw):
            if (b + 1) % ga == 0:
                g0 = b + 1 - ga
                r0 = pl.multiple_of((kb0 + g0) * t, ga * t)
                s_scr[g0 * t:(b + 1) * t, :] = jnp.dot(
                    kaug[pl.ds(r0, ga * t), :], w_ref[iw], preferred_element_type=f32)

        def inner(j, carry):
            for b in range(cb):
                carry = absorb(s_scr[slot(b), :], j * cb + b, None, carry)
                refill(b, (j + 1) * cb, i)
            return carry

        carry = lax.fori_loop(0, n_chunks - 1, inner, init)
        for b in range(cb):
            carry = absorb(s_scr[slot(b), :], (n_chunks - 1) * cb + b, None, carry)
            refill(b, 0, nxt)
        mx, l, acc = own_block(i, None, carry)

        top = jnp.max(mx, axis=0, keepdims=True)
        e = top * c_exp
        safe = jnp.logical_and(jnp.max(e) <= EXP2_SAFE_HI, jnp.min(e) >= EXP2_SAFE_LO)

        def shifted():
            def chunk(j, carry):
                for b in range(cb):
                    carry = absorb(scores(j * cb + b, i), j * cb + b, top, carry)
                return carry
            return own_block(i, top, lax.fori_loop(0, n_chunks, chunk, init))[1:]

        l, acc = lax.cond(safe, lambda: (l, acc), shifted)
        o = (acc / jnp.sum(l, axis=0, keepdims=True)).T
        rows = pl.ds(pl.multiple_of(i * t, t), t)
        o_ref[rows, :] = (o * _silu(z_ref[rows, :].astype(f32))).astype(o_ref.dtype)
        return 0

    lax.fori_loop(0, nb, query_block, 0)


def _moba_attention(qz, kv, km, *, cb=8, ga=8):
    s = qz.shape[0]
    d = N_HEADS * HEAD_DIM
    t = MOBA_BLOCK
    h_ = N_HEADS
    nb = s // t
    cb = min(cb, nb)
    ga = min(ga, cb)
    assert nb % cb == 0 and cb % ga == 0
    w_all, vt = _moba_prep(qz, kv, km)
    c_exp = (1.0 / math.sqrt(HEAD_DIM)) * math.log2(math.e)

    def once(shape, index_map):
        return pl.BlockSpec(shape, index_map, pipeline_mode=pl.Buffered(1))

    return pl.pallas_call(
        functools.partial(_moba_kernel, cb=cb, ga=ga, c_exp=c_exp),
        grid=(h_,),
        in_specs=[
            pl.BlockSpec((None, nb, HEAD_DIM + LANES, t), lambda h: (h, 0, 0, 0)),
            once((s, HEAD_DIM), lambda h: (0, h)),
            pl.BlockSpec((None, nb, HEAD_DIM, t), lambda h: (h, 0, 0, 0)),
            once((s, HEAD_DIM), lambda h: (0, h_ + h)),
        ],
        out_specs=pl.BlockSpec((s, HEAD_DIM), lambda h: (0, h)),
        out_shape=jax.ShapeDtypeStruct((s, d), MXU_DTYPE),
        scratch_shapes=[pltpu.VMEM((s, HEAD_DIM + LANES), MXU_DTYPE),
                        pltpu.VMEM((cb * t, t), jnp.float32)],
        compiler_params=pltpu.CompilerParams(
            dimension_semantics=("arbitrary",),
            vmem_limit_bytes=VMEM_LIMIT_BYTES),
        name="moba_attention",
    )(w_all, kv, vt, qz)


def _post_kernel(a_ref, x_ref, p_ref, wo_ref, g_ref, b_ref, wp_ref, wg_ref, o_ref, *, alpha):
    y = jnp.dot(a_ref[...], wo_ref[...], preferred_element_type=jnp.float32)
    xn = _layer_norm(alpha * x_ref[...] + y, g_ref[...], b_ref[...])
    gate = jax.nn.sigmoid(jnp.dot(xn.astype(MXU_DTYPE), wg_ref[...],
                                  preferred_element_type=jnp.float32))
    e = jnp.dot(p_ref[...].astype(MXU_DTYPE), wp_ref[...], preferred_element_type=jnp.float32)
    o_ref[...] = xn + e * gate


def _post(a, x, p, w_out, g, b, w_ple, w_gate, *, alpha, tm=512):
    s, d = x.shape
    pd = p.shape[1]
    tm = min(tm, s)
    assert s % tm == 0

    def const(shape):
        return pl.BlockSpec(shape, lambda m: (0, 0), pipeline_mode=pl.Buffered(1))

    return pl.pallas_call(
        functools.partial(_post_kernel, alpha=alpha),
        grid=(s // tm,),
        in_specs=[
            pl.BlockSpec((tm, d), lambda m: (m, 0)),
            pl.BlockSpec((tm, d), lambda m: (m, 0)),
            pl.BlockSpec((tm, pd), lambda m: (m, 0)),
            const((d, d)),
            const((1, d)),
            const((1, d)),
            const((pd, d)),
            const((d, d)),
        ],
        out_specs=pl.BlockSpec((tm, d), lambda m: (m, 0)),
        out_shape=jax.ShapeDtypeStruct((s, d), jnp.float32),
        compiler_params=pltpu.CompilerParams(
            dimension_semantics=("arbitrary",),
            vmem_limit_bytes=VMEM_LIMIT_BYTES),
        name="post",
    )(a, x, p, w_out, g.reshape(1, d), b.reshape(1, d), w_ple, w_gate)


def _rope_tables(s):
    half = HEAD_DIM // 2
    inv_freq = ROPE_THETA ** (-jnp.arange(half, dtype=jnp.float32) / half)
    ang = jnp.arange(s, dtype=jnp.int32).astype(jnp.float32)[:, None] * inv_freq[None, :]
    cos = jnp.cos(ang)
    sin = jnp.sin(ang)
    return jnp.concatenate([cos, cos], axis=1), jnp.concatenate([-sin, sin], axis=1)


def kernel(x, p, w_in_a, w_out_a, w_kv, ln_kv_g, ln_kv_b, w_in_b, w_out_b, ln_g, ln_b,
           w_ple, w_ple_gate):
    bsz, s, d = x.shape
    depth = p.shape[0]
    n_a = w_in_a.shape[0]
    assert d == N_HEADS * HEAD_DIM and s % MOBA_BLOCK == 0
    alpha = (2.0 * depth) ** 0.25
    cast = lambda w: w.astype(MXU_DTYPE)
    rope = _rope_tables(s)
    nb = s // MOBA_BLOCK
    nbr = -(-nb // SUBLANES) * SUBLANES

    outs = []
    for bi in range(bsz):
        xb = x[bi]
        kv = km = None
        for i in range(depth):
            if i < n_a:
                qkvz = _proj(xb, cast(w_in_a[i]))
                a = _sb_attention(qkvz)
                w_out = w_out_a[i]
            else:
                if i == n_a:
                    kv, km3 = _proj(xb, cast(w_kv), ln=(ln_kv_g, ln_kv_b), rope=rope,
                                    rope_cols=d, with_kmean=True)
                    km = km3.reshape(nb, 2 * d)
                    km = jnp.pad(km, ((0, nbr - nb), (0, 0)))
                qz = _proj(xb, cast(w_in_b[i - n_a]), rope=rope, rope_cols=d)
                a = _moba_attention(qz, kv, km)
                w_out = w_out_b[i - n_a]
            xb = _post(a, xb, p[i, bi], cast(w_out), ln_g[i], ln_b[i], cast(w_ple[i]),
                       cast(w_ple_gate[i]), alpha=alpha)
        outs.append(xb)
    return jnp.stack(outs, axis=0)
```

```python
import functools
import math

import jax
import jax.numpy as jnp
from jax import lax
from jax.experimental import pallas as pl
from jax.experimental.pallas import tpu as pltpu

N_HEADS = 16
HEAD_DIM = 128
ROPE_THETA = 10000.0
MOBA_BLOCK = 256
MOBA_TOPK = 3
LN_EPS = 1e-5
NEG_BIG = -1e30

LANES = 128
SUBLANES = 8
MXU_DTYPE = jnp.bfloat16
VMEM_LIMIT_BYTES = 60 * 1024 * 1024

SB_SKIP_LOG = 104.0

_NT = (((1,), (1,)), ((), ()))


def _layer_norm(x, g, b):
    mu = jnp.mean(x, axis=-1, keepdims=True)
    xc = x - mu
    var = jnp.mean(xc * xc, axis=-1, keepdims=True)
    return xc * lax.rsqrt(var + LN_EPS) * g + b


def _silu(z):
    return z * jax.nn.sigmoid(z)


def _proj_kernel(*refs, ln, rope_tiles, with_kmean, tm, tn):
    refs = list(refs)
    x_ref = refs.pop(0)
    if ln:
        g_ref = refs.pop(0)
        b_ref = refs.pop(0)
    if rope_tiles:
        cos_ref = refs.pop(0)
        sin_ref = refs.pop(0)
    w_ref = refs.pop(0)
    o_ref = refs.pop(0)
    if with_kmean:
        km_ref = refs.pop(0)
    a_scr = refs.pop(0)

    n = pl.program_id(1)

    @pl.when(n == 0)
    def _():
        xv = x_ref[...]
        if ln:
            xv = _layer_norm(xv, g_ref[...], b_ref[...])
        a_scr[...] = xv.astype(a_scr.dtype)

    def matmul(cols=slice(None)):
        return jnp.dot(a_scr[...], w_ref[:, cols], preferred_element_type=jnp.float32)

    def write_plain():
        o_ref[...] = matmul().astype(o_ref.dtype)
        if with_kmean:
            km_ref[...] = jnp.zeros(km_ref.shape, km_ref.dtype)

    def write_rope():
        cos2 = cos_ref[...]
        sin2 = sin_ref[...]
        acc = matmul()
        for j in range(tn // HEAD_DIM):
            t = acc[:, j * HEAD_DIM:(j + 1) * HEAD_DIM]
            r = t * cos2 + pltpu.roll(t, HEAD_DIM // 2, axis=1) * sin2
            o_ref[:, j * HEAD_DIM:(j + 1) * HEAD_DIM] = r.astype(o_ref.dtype)
            if with_kmean:
                for blk in range(tm // MOBA_BLOCK):
                    rows = r[blk * MOBA_BLOCK:(blk + 1) * MOBA_BLOCK, :]
                    km_ref[blk, :, j * HEAD_DIM:(j + 1) * HEAD_DIM] = (
                        jnp.sum(rows, axis=0, keepdims=True) * (1.0 / MOBA_BLOCK))

    if rope_tiles:
        pl.when(n < rope_tiles)(write_rope)
        pl.when(n >= rope_tiles)(write_plain)
    else:
        write_plain()


def _proj(x, w, *, ln=None, rope=None, rope_cols=0, with_kmean=False, tm=1024, tn=1024):
    s, d = x.shape
    n_out = w.shape[1]
    tm = min(tm, s)
    assert s % tm == 0 and n_out % tn == 0 and tn % HEAD_DIM == 0 and rope_cols % tn == 0
    assert tm % MOBA_BLOCK == 0
    rope_tiles = rope_cols // tn
    in_specs = [pl.BlockSpec((tm, d), lambda m, n: (m, 0))]
    args = [x]
    if ln is not None:
        in_specs += [pl.BlockSpec((1, d), lambda m, n: (0, 0))] * 2
        args += [ln[0].reshape(1, d), ln[1].reshape(1, d)]
    if rope_tiles:
        in_specs += [pl.BlockSpec((tm, HEAD_DIM), lambda m, n: (m, 0))] * 2
        args += [rope[0], rope[1]]
    in_specs.append(pl.BlockSpec((d, tn), lambda m, n: (0, n)))
    args.append(w)
    out_shape = [jax.ShapeDtypeStruct((s, n_out), MXU_DTYPE)]
    out_specs = [pl.BlockSpec((tm, tn), lambda m, n: (m, n))]
    if with_kmean:
        out_shape.append(jax.ShapeDtypeStruct((s // MOBA_BLOCK, 1, n_out), jnp.float32))
        out_specs.append(pl.BlockSpec((tm // MOBA_BLOCK, 1, tn), lambda m, n: (m, 0, n)))
    res = pl.pallas_call(
        functools.partial(_proj_kernel, ln=ln is not None, rope_tiles=rope_tiles,
                          with_kmean=with_kmean, tm=tm, tn=tn),
        grid=(s // tm, n_out // tn),
        in_specs=in_specs,
        out_specs=out_specs,
        out_shape=out_shape,
        scratch_shapes=[pltpu.VMEM((tm, d), MXU_DTYPE)],
        compiler_params=pltpu.CompilerParams(
            dimension_semantics=("arbitrary", "arbitrary"),
            vmem_limit_bytes=VMEM_LIMIT_BYTES),
        name="proj",
    )(*args)
    return res if with_kmean else res[0]


def _sb_kernel(q_ref, k_ref, v_ref, z_ref, o_ref, *, t, scale, hp, nq):
    row = lax.broadcasted_iota(jnp.int32, (t, t), 0)
    col = lax.broadcasted_iota(jnp.int32, (t, t), 1)
    tri = (row > col).astype(MXU_DTYPE)
    past = col < row
    pairs = [_SbHead(q_ref, k_ref, v_ref, slice(qq * t, (qq + 1) * t),
                     slice(hh * HEAD_DIM, (hh + 1) * HEAD_DIM), t, scale, tri)
             for qq in range(nq) for hh in range(hp)]
    blocks = [pl.program_id(1) * nq + qq for qq in range(nq) for _ in range(hp)]
    first = [pr.first_two(i, jnp.maximum(i - 1, 0), past, i > 0)
             for pr, i in zip(pairs, blocks)]
    for pr, i, (c, acc) in zip(pairs, blocks, first):
        acc = pr.walk(i - 2, c, acc)
        z = z_ref[pr.rows, pr.cols].astype(jnp.float32)
        o_ref[pr.rows, pr.cols] = (acc * _silu(z)).astype(o_ref.dtype)


class _SbHead:
    def __init__(self, q_ref, k_ref, v_ref, rows, cols, t, scale, tri):
        self.q = q_ref[rows, cols]
        self.k_ref, self.v_ref, self.rows, self.cols = k_ref, v_ref, rows, cols
        self.t, self.scale, self.tri = t, scale, tri

    def logits(self, kb, keep):
        t, tri = self.t, self.tri
        start = pl.multiple_of(kb * t, t)
        s = lax.dot_general(self.q, self.k_ref[pl.ds(start, t), self.cols], _NT,
                            preferred_element_type=jnp.float32) * self.scale
        sp = jnp.log(1.0 + jnp.exp(-jnp.abs(s)))
        log_beta = jnp.minimum(s, 0.0) - sp
        l1m = -jnp.maximum(s, 0.0) - sp
        if keep is not None:
            l1m = jnp.where(keep, l1m, 0.0)
        hi = l1m.astype(MXU_DTYPE)
        lo = (l1m - hi.astype(jnp.float32)).astype(MXU_DTYPE)
        tail = (jnp.dot(hi, tri, preferred_element_type=jnp.float32)
                + jnp.dot(lo, tri, preferred_element_type=jnp.float32))
        return log_beta, tail, jnp.sum(l1m, axis=1, keepdims=True)

    def weighted(self, kb, log_beta, tail, c, keep):
        t = self.t
        start = pl.multiple_of(kb * t, t)
        w = jnp.exp(log_beta + tail + c)
        if keep is not None:
            w = jnp.where(keep, w, 0.0)
        return jnp.dot(w.astype(MXU_DTYPE), self.v_ref[pl.ds(start, t), self.cols],
                       preferred_element_type=jnp.float32)

    def first_two(self, i, prev, past, has_prev):
        lb_d, tail_d, sum_d = self.logits(i, past)
        lb_p, tail_p, sum_p = self.logits(prev, None)
        acc_p = self.weighted(prev, lb_p, tail_p, sum_d, None)
        acc = self.weighted(i, lb_d, tail_d, 0.0, past) + jnp.where(has_prev, acc_p, 0.0)
        return sum_d + jnp.where(has_prev, sum_p, 0.0), acc

    def walk(self, kb0, c, acc):
        def cond(carry):
            kb, c, _ = carry
            return jnp.logical_and(kb >= 0, jnp.max(c) > -SB_SKIP_LOG)

        def body(carry):
            kb, c, acc = carry
            lb, tail, tot = self.logits(kb, None)
            return kb - 1, c + tot, acc + self.weighted(kb, lb, tail, c, None)

        return lax.while_loop(cond, body, (kb0, c, acc))[2]


def _sb_attention(qkvz, *, t=256, hp=2, nq=2):
    s = qkvz.shape[0]
    d = N_HEADS * HEAD_DIM
    nq = min(nq, s // t)
    assert s % (nq * t) == 0 and N_HEADS % hp == 0
    h_ = N_HEADS // hp
    wd = hp * HEAD_DIM
    return pl.pallas_call(
        functools.partial(_sb_kernel, t=t, scale=1.0 / math.sqrt(HEAD_DIM), hp=hp, nq=nq),
        grid=(h_, s // (nq * t)),
        in_specs=[
            pl.BlockSpec((nq * t, wd), lambda h, i: (i, h)),
            pl.BlockSpec((s, wd), lambda h, i: (0, h_ + h)),
            pl.BlockSpec((s, wd), lambda h, i: (0, 2 * h_ + h)),
            pl.BlockSpec((nq * t, wd), lambda h, i: (i, 3 * h_ + h)),
        ],
        out_specs=pl.BlockSpec((nq * t, wd), lambda h, i: (i, h)),
        out_shape=jax.ShapeDtypeStruct((s, d), MXU_DTYPE),
        compiler_params=pltpu.CompilerParams(
            dimension_semantics=("arbitrary", "arbitrary"),
            vmem_limit_bytes=VMEM_LIMIT_BYTES),
        name="sb_attention",
    )(qkvz, qkvz, qkvz, qkvz)


MASK_BIG = 2.0 ** 100
EXP2_SAFE_HI = 100.0
EXP2_SAFE_LO = -60.0


def _moba_prep_kernel(q_ref, v_ref, km_ref, w_ref, vt_ref, *, g):
    t = MOBA_BLOCK
    f32 = jnp.float32
    nbr = km_ref.shape[0]
    kmb = km_ref[...].astype(MXU_DTYPE)
    blk = lax.broadcasted_iota(jnp.int32, (nbr, t), 0)
    for gg in range(g):
        i = pl.program_id(1) * g + gg
        rows = slice(gg * t, (gg + 1) * t)
        qt = q_ref[rows, :].astype(f32).T.astype(MXU_DTYPE)
        gate = jnp.dot(kmb, qt, preferred_element_type=f32)
        valid = blk < i
        gv = jnp.where(valid, gate, NEG_BIG)
        sel = jnp.zeros((nbr, t), f32)
        for _ in range(MOBA_TOPK):
            mx = jnp.max(gv, axis=0, keepdims=True)
            first = jnp.min(jnp.where(gv == mx, blk, nbr), axis=0, keepdims=True)
            pick = blk == first
            sel = jnp.where(pick, 1.0, sel)
            gv = jnp.where(pick, -jnp.inf, gv)
        bias = jnp.where(jnp.logical_and(valid, sel > 0.0), 0.0, -MASK_BIG)
        w_ref[gg, 0:HEAD_DIM, :] = qt
        w_ref[gg, HEAD_DIM:HEAD_DIM + nbr, :] = bias.astype(w_ref.dtype)
        if nbr < LANES:
            w_ref[gg, HEAD_DIM + nbr:, :] = jnp.zeros((LANES - nbr, t), w_ref.dtype)
        vt_ref[gg] = v_ref[rows, :].astype(f32).T.astype(vt_ref.dtype)


def _moba_prep(qz, kv, km, *, g=16):
    s = qz.shape[0]
    t = MOBA_BLOCK
    h_ = N_HEADS
    nb = s // t
    nbr = km.shape[0]
    g = min(g, nb)
    assert nb % g == 0 and nbr % SUBLANES == 0 and nb <= nbr <= LANES
    return pl.pallas_call(
        functools.partial(_moba_prep_kernel, g=g),
        grid=(h_, nb // g),
        in_specs=[
            pl.BlockSpec((g * t, HEAD_DIM), lambda h, i: (i, h)),
            pl.BlockSpec((g * t, HEAD_DIM), lambda h, i: (i, h_ + h)),
            pl.BlockSpec((nbr, HEAD_DIM), lambda h, i: (0, h)),
        ],
        out_specs=[
            pl.BlockSpec((None, g, HEAD_DIM + LANES, t), lambda h, i: (h, i, 0, 0)),
            pl.BlockSpec((None, g, HEAD_DIM, t), lambda h, i: (h, i, 0, 0)),
        ],
        out_shape=[
            jax.ShapeDtypeStruct((h_, nb, HEAD_DIM + LANES, t), MXU_DTYPE),
            jax.ShapeDtypeStruct((h_, nb, HEAD_DIM, t), MXU_DTYPE),
        ],
        compiler_params=pltpu.CompilerParams(
            dimension_semantics=("arbitrary", "arbitrary"),
            vmem_limit_bytes=VMEM_LIMIT_BYTES),
        name="moba_prep",
    )(qz, kv, km)


def _moba_kernel(w_ref, k_ref, vt_ref, z_ref, o_ref, kaug, s_scr, *, cb, ga, c_exp):
    t = MOBA_BLOCK
    nb = vt_ref.shape[0]
    f32 = jnp.float32

    @pl.when(pl.program_id(0) == 0)
    def _():
        lane = lax.broadcasted_iota(jnp.int32, (t, LANES), 1)

        def fill(b, carry):
            r0 = pl.multiple_of(b * t, t)
            kaug[pl.ds(r0, t), HEAD_DIM:HEAD_DIM + LANES] = (lane == b).astype(kaug.dtype)
            return carry

        lax.fori_loop(0, nb, fill, 0)

    kaug[:, 0:HEAD_DIM] = k_ref[...]

    key = lax.broadcasted_iota(jnp.int32, (t, t), 0)
    qry = lax.broadcasted_iota(jnp.int32, (t, t), 1)
    causal = key <= qry

    def fold(x, op):
        return op(x.reshape(t // SUBLANES, SUBLANES, t), axis=0)

    def scores(kb, i):
        r0 = pl.multiple_of(kb * t, t)
        return jnp.dot(kaug[pl.ds(r0, t), :], w_ref[i], preferred_element_type=f32)

    def absorb(s, kb, shift, carry):
        mx, l, acc = carry
        mx = jnp.maximum(mx, fold(s, jnp.max))
        p = jnp.exp2((s if shift is None else s - shift) * c_exp)
        l = l + fold(p, jnp.sum)
        acc = acc + jnp.dot(vt_ref[kb], p.astype(MXU_DTYPE), preferred_element_type=f32)
        return mx, l, acc

    def own_block(i, shift, carry):
        own = pl.multiple_of(i * t, t)
        s = jnp.dot(k_ref[pl.ds(own, t), :], w_ref[i, 0:HEAD_DIM, :],
                    preferred_element_type=f32)
        return absorb(jnp.where(causal, s, -MASK_BIG), i, shift, carry)

    init = (jnp.full((SUBLANES, t), -MASK_BIG, f32), jnp.zeros((SUBLANES, t), f32),
            jnp.zeros((HEAD_DIM, t), f32))

    def slot(b):
        return slice(b * t, (b + 1) * t)

    for b in range(cb):
        s_scr[slot(b), :] = scores(b, 0)

    def query_block(i, _):
        n_chunks = jnp.maximum((i + cb - 1) // cb, 1)
        nxt = jnp.minimum(i + 1, nb - 1)

        def refill(b, kb0, iw):
            if (b + 1) % ga == 0:
                g0 = b + 1 - ga
                r0 = pl.multiple_of((kb0 + g0) * t, ga * t)
                s_scr[g0 * t:(b + 1) * t, :] = jnp.dot(
                    kaug[pl.ds(r0, ga * t), :], w_ref[iw], preferred_element_type=f32)

        def inner(j, carry):
            for b in range(cb):
                carry = absorb(s_scr[slot(b), :], j * cb + b, None, carry)
                refill(b, (j + 1) * cb, i)
            return carry

        carry = lax.fori_loop(0, n_chunks - 1, inner, init)
        for b in range(cb):
            carry = absorb(s_scr[slot(b), :], (n_chunks - 1) * cb + b, None, carry)
            refill(b, 0, nxt)
        mx, l, acc = own_block(i, None, carry)

        top = jnp.max(mx, axis=0, keepdims=True)
        e = top * c_exp
        safe = jnp.logical_and(jnp.max(e) <= EXP2_SAFE_HI, jnp.min(e) >= EXP2_SAFE_LO)

        def shifted():
            def block(kb, carry):
                return absorb(scores(kb, i), kb, top, carry)
            return own_block(i, top, lax.fori_loop(0, n_chunks * cb, block, init))[1:]

        l, acc = lax.cond(safe, lambda: (l, acc), shifted)
        o = (acc / jnp.sum(l, axis=0, keepdims=True)).T
        rows = pl.ds(pl.multiple_of(i * t, t), t)
        o_ref[rows, :] = (o * _silu(z_ref[rows, :].astype(f32))).astype(o_ref.dtype)
        return 0

    lax.fori_loop(0, nb, query_block, 0)


def _moba_attention(qz, kv, km, *, cb=8, ga=8):
    s = qz.shape[0]
    d = N_HEADS * HEAD_DIM
    t = MOBA_BLOCK
    h_ = N_HEADS
    nb = s // t
    cb = min(cb, nb)
    ga = min(ga, cb)
    assert nb % cb == 0 and cb % ga == 0
    w_all, vt = _moba_prep(qz, kv, km)
    c_exp = (1.0 / math.sqrt(HEAD_DIM)) * math.log2(math.e)

    def once(shape, index_map):
        return pl.BlockSpec(shape, index_map, pipeline_mode=pl.Buffered(1))

    return pl.pallas_call(
        functools.partial(_moba_kernel, cb=cb, ga=ga, c_exp=c_exp),
        grid=(h_,),
        in_specs=[
            pl.BlockSpec((None, nb, HEAD_DIM + LANES, t), lambda h: (h, 0, 0, 0)),
            once((s, HEAD_DIM), lambda h: (0, h)),
            pl.BlockSpec((None, nb, HEAD_DIM, t), lambda h: (h, 0, 0, 0)),
            once((s, HEAD_DIM), lambda h: (0, h_ + h)),
        ],
        out_specs=pl.BlockSpec((s, HEAD_DIM), lambda h: (0, h)),
        out_shape=jax.ShapeDtypeStruct((s, d), MXU_DTYPE),
        scratch_shapes=[pltpu.VMEM((s, HEAD_DIM + LANES), MXU_DTYPE),
                        pltpu.VMEM((cb * t, t), jnp.float32)],
        compiler_params=pltpu.CompilerParams(
            dimension_semantics=("arbitrary",),
            vmem_limit_bytes=VMEM_LIMIT_BYTES),
        name="moba_attention",
    )(w_all, kv, vt, qz)


def _post_kernel(a_ref, x_ref, p_ref, wo_ref, g_ref, b_ref, wp_ref, wg_ref, o_ref, *, alpha):
    y = jnp.dot(a_ref[...], wo_ref[...], preferred_element_type=jnp.float32)
    xn = _layer_norm(alpha * x_ref[...] + y, g_ref[...], b_ref[...])
    gate = jax.nn.sigmoid(jnp.dot(xn.astype(MXU_DTYPE), wg_ref[...],
                                  preferred_element_type=jnp.float32))
    e = jnp.dot(p_ref[...].astype(MXU_DTYPE), wp_ref[...], preferred_element_type=jnp.float32)
    o_ref[...] = xn + e * gate


def _post(a, x, p, w_out, g, b, w_ple, w_gate, *, alpha, tm=512):
    s, d = x.shape
    pd = p.shape[1]
    tm = min(tm, s)
    assert s % tm == 0

    def const(shape):
        return pl.BlockSpec(shape, lambda m: (0, 0), pipeline_mode=pl.Buffered(1))

    return pl.pallas_call(
        functools.partial(_post_kernel, alpha=alpha),
        grid=(s // tm,),
        in_specs=[
            pl.BlockSpec((tm, d), lambda m: (m, 0)),
            pl.BlockSpec((tm, d), lambda m: (m, 0)),
            pl.BlockSpec((tm, pd), lambda m: (m, 0)),
            const((d, d)),
            const((1, d)),
            const((1, d)),
            const((pd, d)),
            const((d, d)),
        ],
        out_specs=pl.BlockSpec((tm, d), lambda m: (m, 0)),
        out_shape=jax.ShapeDtypeStruct((s, d), jnp.float32),
        compiler_params=pltpu.CompilerParams(
            dimension_semantics=("arbitrary",),
            vmem_limit_bytes=VMEM_LIMIT_BYTES),
        name="post",
    )(a, x, p, w_out, g.reshape(1, d), b.reshape(1, d), w_ple, w_gate)


def _rope_tables(s):
    half = HEAD_DIM // 2
    inv_freq = ROPE_THETA ** (-jnp.arange(half, dtype=jnp.float32) / half)
    ang = jnp.arange(s, dtype=jnp.int32).astype(jnp.float32)[:, None] * inv_freq[None, :]
    cos = jnp.cos(ang)
    sin = jnp.sin(ang)
    return jnp.concatenate([cos, cos], axis=1), jnp.concatenate([-sin, sin], axis=1)


def kernel(x, p, w_in_a, w_out_a, w_kv, ln_kv_g, ln_kv_b, w_in_b, w_out_b, ln_g, ln_b,
           w_ple, w_ple_gate):
    bsz, s, d = x.shape
    depth = p.shape[0]
    n_a = w_in_a.shape[0]
    assert d == N_HEADS * HEAD_DIM and s % MOBA_BLOCK == 0
    alpha = (2.0 * depth) ** 0.25
    cast = lambda w: w.astype(MXU_DTYPE)
    rope = _rope_tables(s)
    nb = s // MOBA_BLOCK
    nbr = -(-nb // SUBLANES) * SUBLANES

    outs = []
    for bi in range(bsz):
        xb = x[bi]
        kv = km = None
        for i in range(depth):
            if i < n_a:
                qkvz = _proj(xb, cast(w_in_a[i]))
                a = _sb_attention(qkvz)
                w_out = w_out_a[i]
            else:
                if i == n_a:
                    kv, km3 = _proj(xb, cast(w_kv), ln=(ln_kv_g, ln_kv_b), rope=rope,
                                    rope_cols=d, with_kmean=True)
                    km = km3.reshape(nb, 2 * d)
                    km = jnp.pad(km, ((0, nbr - nb), (0, 0)))
                qz = _proj(xb, cast(w_in_b[i - n_a]), rope=rope, rope_cols=d)
                a = _moba_attention(qz, kv, km)
                w_out = w_out_b[i - n_a]
            xb = _post(a, xb, p[i, bi], cast(w_out), ln_g[i], ln_b[i], cast(w_ple[i]),
                       cast(w_ple_gate[i]), alpha=alpha)
        outs.append(xb)
    return jnp.stack(outs, axis=0)
```

```python
import functools
import math

import jax
import jax.numpy as jnp
from jax import lax
from jax.experimental import pallas as pl
from jax.experimental.pallas import tpu as pltpu

N_HEADS = 16
HEAD_DIM = 128
ROPE_THETA = 10000.0
MOBA_BLOCK = 256
MOBA_TOPK = 3
LN_EPS = 1e-5
NEG_BIG = -1e30

LANES = 128
SUBLANES = 8
MXU_DTYPE = jnp.bfloat16
VMEM_LIMIT_BYTES = 60 * 1024 * 1024

SB_SKIP_LOG = 104.0

_NT = (((1,), (1,)), ((), ()))


def _layer_norm(x, g, b):
    mu = jnp.mean(x, axis=-1, keepdims=True)
    xc = x - mu
    var = jnp.mean(xc * xc, axis=-1, keepdims=True)
    return xc * lax.rsqrt(var + LN_EPS) * g + b


def _silu(z):
    return z * jax.nn.sigmoid(z)


def _proj_kernel(*refs, ln, rope_tiles, with_kmean, tm, tn):
    refs = list(refs)
    x_ref = refs.pop(0)
    if ln:
        g_ref = refs.pop(0)
        b_ref = refs.pop(0)
    if rope_tiles:
        cos_ref = refs.pop(0)
        sin_ref = refs.pop(0)
    w_ref = refs.pop(0)
    o_ref = refs.pop(0)
    if with_kmean:
        km_ref = refs.pop(0)
    a_scr = refs.pop(0)

    n = pl.program_id(1)

    @pl.when(n == 0)
    def _():
        xv = x_ref[...]
        if ln:
            xv = _layer_norm(xv, g_ref[...], b_ref[...])
        a_scr[...] = xv.astype(a_scr.dtype)

    def matmul(cols=slice(None)):
        return jnp.dot(a_scr[...], w_ref[:, cols], preferred_element_type=jnp.float32)

    def write_plain():
        o_ref[...] = matmul().astype(o_ref.dtype)
        if with_kmean:
            km_ref[...] = jnp.zeros(km_ref.shape, km_ref.dtype)

    def write_rope():
        cos2 = cos_ref[...]
        sin2 = sin_ref[...]
        acc = matmul()
        for j in range(tn // HEAD_DIM):
            t = acc[:, j * HEAD_DIM:(j + 1) * HEAD_DIM]
            r = t * cos2 + pltpu.roll(t, HEAD_DIM // 2, axis=1) * sin2
            o_ref[:, j * HEAD_DIM:(j + 1) * HEAD_DIM] = r.astype(o_ref.dtype)
            if with_kmean:
                for blk in range(tm // MOBA_BLOCK):
                    rows = r[blk * MOBA_BLOCK:(blk + 1) * MOBA_BLOCK, :]
                    km_ref[blk, :, j * HEAD_DIM:(j + 1) * HEAD_DIM] = (
                        jnp.sum(rows, axis=0, keepdims=True) * (1.0 / MOBA_BLOCK))

    if rope_tiles:
        pl.when(n < rope_tiles)(write_rope)
        pl.when(n >= rope_tiles)(write_plain)
    else:
        write_plain()


def _proj(x, w, *, ln=None, rope=None, rope_cols=0, with_kmean=False, tm=1024, tn=1024):
    s, d = x.shape
    n_out = w.shape[1]
    tm = min(tm, s)
    assert s % tm == 0 and n_out % tn == 0 and tn % HEAD_DIM == 0 and rope_cols % tn == 0
    assert tm % MOBA_BLOCK == 0
    rope_tiles = rope_cols // tn
    in_specs = [pl.BlockSpec((tm, d), lambda m, n: (m, 0))]
    args = [x]
    if ln is not None:
        in_specs += [pl.BlockSpec((1, d), lambda m, n: (0, 0))] * 2
        args += [ln[0].reshape(1, d), ln[1].reshape(1, d)]
    if rope_tiles:
        in_specs += [pl.BlockSpec((tm, HEAD_DIM), lambda m, n: (m, 0))] * 2
        args += [rope[0], rope[1]]
    in_specs.append(pl.BlockSpec((d, tn), lambda m, n: (0, n)))
    args.append(w)
    out_shape = [jax.ShapeDtypeStruct((s, n_out), MXU_DTYPE)]
    out_specs = [pl.BlockSpec((tm, tn), lambda m, n: (m, n))]
    if with_kmean:
        out_shape.append(jax.ShapeDtypeStruct((s // MOBA_BLOCK, 1, n_out), jnp.float32))
        out_specs.append(pl.BlockSpec((tm // MOBA_BLOCK, 1, tn), lambda m, n: (m, 0, n)))
    res = pl.pallas_call(
        functools.partial(_proj_kernel, ln=ln is not None, rope_tiles=rope_tiles,
                          with_kmean=with_kmean, tm=tm, tn=tn),
        grid=(s // tm, n_out // tn),
        in_specs=in_specs,
        out_specs=out_specs,
        out_shape=out_shape,
        scratch_shapes=[pltpu.VMEM((tm, d), MXU_DTYPE)],
        compiler_params=pltpu.CompilerParams(
            dimension_semantics=("arbitrary", "arbitrary"),
            vmem_limit_bytes=VMEM_LIMIT_BYTES),
        name="proj",
    )(*args)
    return res if with_kmean else res[0]


def _sb_kernel(q_ref, k_ref, v_ref, z_ref, o_ref, *, t, scale, hp, nq):
    row = lax.broadcasted_iota(jnp.int32, (t, t), 0)
    col = lax.broadcasted_iota(jnp.int32, (t, t), 1)
    tri = (row > col).astype(MXU_DTYPE)
    past = col < row
    pairs = [_SbHead(q_ref, k_ref, v_ref, slice(qq * t, (qq + 1) * t),
                     slice(hh * HEAD_DIM, (hh + 1) * HEAD_DIM), t, scale, tri)
             for qq in range(nq) for hh in range(hp)]
    blocks = [pl.program_id(1) * nq + qq for qq in range(nq) for _ in range(hp)]
    first = [pr.first_two(i, jnp.maximum(i - 1, 0), past, i > 0)
             for pr, i in zip(pairs, blocks)]
    for pr, i, (c, acc) in zip(pairs, blocks, first):
        acc = pr.walk(i - 2, c, acc)
        z = z_ref[pr.rows, pr.cols].astype(jnp.float32)
        o_ref[pr.rows, pr.cols] = (acc * _silu(z)).astype(o_ref.dtype)


class _SbHead:
    def __init__(self, q_ref, k_ref, v_ref, rows, cols, t, scale, tri):
        self.q = q_ref[rows, cols]
        self.k_ref, self.v_ref, self.rows, self.cols = k_ref, v_ref, rows, cols
        self.t, self.scale, self.tri = t, scale, tri

    def logits(self, kb, keep):
        t, tri = self.t, self.tri
        start = pl.multiple_of(kb * t, t)
        s = lax.dot_general(self.q, self.k_ref[pl.ds(start, t), self.cols], _NT,
                            preferred_element_type=jnp.float32) * self.scale
        sp = jnp.log(1.0 + jnp.exp(-jnp.abs(s)))
        log_beta = jnp.minimum(s, 0.0) - sp
        l1m = -jnp.maximum(s, 0.0) - sp
        if keep is not None:
            l1m = jnp.where(keep, l1m, 0.0)
        hi = l1m.astype(MXU_DTYPE)
        lo = (l1m - hi.astype(jnp.float32)).astype(MXU_DTYPE)
        tail = (jnp.dot(hi, tri, preferred_element_type=jnp.float32)
                + jnp.dot(lo, tri, preferred_element_type=jnp.float32))
        return log_beta, tail, jnp.sum(l1m, axis=1, keepdims=True)

    def weighted(self, kb, log_beta, tail, c, keep):
        t = self.t
        start = pl.multiple_of(kb * t, t)
        w = jnp.exp(log_beta + tail + c)
        if keep is not None:
            w = jnp.where(keep, w, 0.0)
        return jnp.dot(w.astype(MXU_DTYPE), self.v_ref[pl.ds(start, t), self.cols],
                       preferred_element_type=jnp.float32)

    def first_two(self, i, prev, past, has_prev):
        lb_d, tail_d, sum_d = self.logits(i, past)
        lb_p, tail_p, sum_p = self.logits(prev, None)
        acc_p = self.weighted(prev, lb_p, tail_p, sum_d, None)
        acc = self.weighted(i, lb_d, tail_d, 0.0, past) + jnp.where(has_prev, acc_p, 0.0)
        return sum_d + jnp.where(has_prev, sum_p, 0.0), acc

    def walk(self, kb0, c, acc):
        def cond(carry):
            kb, c, _ = carry
            return jnp.logical_and(kb >= 0, jnp.max(c) > -SB_SKIP_LOG)

        def body(carry):
            kb, c, acc = carry
            lb, tail, tot = self.logits(kb, None)
            return kb - 1, c + tot, acc + self.weighted(kb, lb, tail, c, None)

        return lax.while_loop(cond, body, (kb0, c, acc))[2]


def _sb_attention(qkvz, *, t=256, hp=2, nq=4):
    s = qkvz.shape[0]
    d = N_HEADS * HEAD_DIM
    nq = min(nq, s // t)
    assert s % (nq * t) == 0 and N_HEADS % hp == 0
    h_ = N_HEADS // hp
    wd = hp * HEAD_DIM
    return pl.pallas_call(
        functools.partial(_sb_kernel, t=t, scale=1.0 / math.sqrt(HEAD_DIM), hp=hp, nq=nq),
        grid=(h_, s // (nq * t)),
        in_specs=[
            pl.BlockSpec((nq * t, wd), lambda h, i: (i, h)),
            pl.BlockSpec((s, wd), lambda h, i: (0, h_ + h)),
            pl.BlockSpec((s, wd), lambda h, i: (0, 2 * h_ + h)),
            pl.BlockSpec((nq * t, wd), lambda h, i: (i, 3 * h_ + h)),
        ],
        out_specs=pl.BlockSpec((nq * t, wd), lambda h, i: (i, h)),
        out_shape=jax.ShapeDtypeStruct((s, d), MXU_DTYPE),
        compiler_params=pltpu.CompilerParams(
            dimension_semantics=("arbitrary", "arbitrary"),
            vmem_limit_bytes=VMEM_LIMIT_BYTES),
        name="sb_attention",
    )(qkvz, qkvz, qkvz, qkvz)


MASK_BIG = 2.0 ** 100
DENOM_SAFE_LO = 2.0 ** -60
DENOM_SAFE_HI = 2.0 ** 60


def _moba_prep_kernel(q_ref, v_ref, km_ref, w_ref, vt_ref, *, g):
    t = MOBA_BLOCK
    f32 = jnp.float32
    nbr = km_ref.shape[0]
    kmb = km_ref[...].astype(MXU_DTYPE)
    blk = lax.broadcasted_iota(jnp.int32, (nbr, t), 0)
    for gg in range(g):
        i = pl.program_id(1) * g + gg
        rows = slice(gg * t, (gg + 1) * t)
        qt = q_ref[rows, :].astype(f32).T.astype(MXU_DTYPE)
        gate = jnp.dot(kmb, qt, preferred_element_type=f32)
        valid = blk < i
        gv = jnp.where(valid, gate, NEG_BIG)
        sel = jnp.zeros((nbr, t), f32)
        for _ in range(MOBA_TOPK):
            mx = jnp.max(gv, axis=0, keepdims=True)
            first = jnp.min(jnp.where(gv == mx, blk, nbr), axis=0, keepdims=True)
            pick = blk == first
            sel = jnp.where(pick, 1.0, sel)
            gv = jnp.where(pick, -jnp.inf, gv)
        bias = jnp.where(jnp.logical_and(valid, sel > 0.0), 0.0, -MASK_BIG)
        w_ref[gg, 0:HEAD_DIM, :] = qt
        w_ref[gg, HEAD_DIM:HEAD_DIM + nbr, :] = bias.astype(w_ref.dtype)
        if nbr < LANES:
            w_ref[gg, HEAD_DIM + nbr:, :] = jnp.zeros((LANES - nbr, t), w_ref.dtype)
        vt_ref[gg] = v_ref[rows, :].astype(f32).T.astype(vt_ref.dtype)


def _moba_prep(qz, kv, km, *, g=16):
    s = qz.shape[0]
    t = MOBA_BLOCK
    h_ = N_HEADS
    nb = s // t
    nbr = km.shape[0]
    g = min(g, nb)
    assert nb % g == 0 and nbr % SUBLANES == 0 and nb <= nbr <= LANES
    return pl.pallas_call(
        functools.partial(_moba_prep_kernel, g=g),
        grid=(h_, nb // g),
        in_specs=[
            pl.BlockSpec((g * t, HEAD_DIM), lambda h, i: (i, h)),
            pl.BlockSpec((g * t, HEAD_DIM), lambda h, i: (i, h_ + h)),
            pl.BlockSpec((nbr, HEAD_DIM), lambda h, i: (0, h)),
        ],
        out_specs=[
            pl.BlockSpec((None, g, HEAD_DIM + LANES, t), lambda h, i: (h, i, 0, 0)),
            pl.BlockSpec((None, g, HEAD_DIM, t), lambda h, i: (h, i, 0, 0)),
        ],
        out_shape=[
            jax.ShapeDtypeStruct((h_, nb, HEAD_DIM + LANES, t), MXU_DTYPE),
            jax.ShapeDtypeStruct((h_, nb, HEAD_DIM, t), MXU_DTYPE),
        ],
        compiler_params=pltpu.CompilerParams(
            dimension_semantics=("arbitrary", "arbitrary"),
            vmem_limit_bytes=VMEM_LIMIT_BYTES),
        name="moba_prep",
    )(qz, kv, km)


def _moba_kernel(w_ref, k_ref, vt_ref, z_ref, o_ref, kaug, s_scr, *, cb, ga, c_exp):
    t = MOBA_BLOCK
    nb = vt_ref.shape[0]
    f32 = jnp.float32

    @pl.when(pl.program_id(0) == 0)
    def _():
        lane = lax.broadcasted_iota(jnp.int32, (t, LANES), 1)

        def fill(b, carry):
            r0 = pl.multiple_of(b * t, t)
            kaug[pl.ds(r0, t), HEAD_DIM:HEAD_DIM + LANES] = (lane == b).astype(kaug.dtype)
            return carry

        lax.fori_loop(0, nb, fill, 0)

    kaug[:, 0:HEAD_DIM] = k_ref[...]

    key = lax.broadcasted_iota(jnp.int32, (t, t), 0)
    qry = lax.broadcasted_iota(jnp.int32, (t, t), 1)
    causal = key <= qry

    def fold(x, op):
        return op(x.reshape(t // SUBLANES, SUBLANES, t), axis=0)

    def scores(kb, i):
        r0 = pl.multiple_of(kb * t, t)
        return jnp.dot(kaug[pl.ds(r0, t), :], w_ref[i], preferred_element_type=f32)

    def absorb(s, kb, shift, carry):
        l, acc = carry
        p = jnp.exp2((s if shift is None else s - shift) * c_exp)
        l = l + fold(p, jnp.sum)
        acc = acc + jnp.dot(vt_ref[kb], p.astype(MXU_DTYPE), preferred_element_type=f32)
        return l, acc

    def own_scores(i):
        own = pl.multiple_of(i * t, t)
        s = jnp.dot(k_ref[pl.ds(own, t), :], w_ref[i, 0:HEAD_DIM, :],
                    preferred_element_type=f32)
        return jnp.where(causal, s, -MASK_BIG)

    def finish(i, l, acc):
        o = (acc / l).T
        rows = pl.ds(pl.multiple_of(i * t, t), t)
        o_ref[rows, :] = (o * _silu(z_ref[rows, :].astype(f32))).astype(o_ref.dtype)
        amax = jnp.max(jnp.abs(acc), axis=0, keepdims=True)
        ok = jnp.logical_and(jnp.logical_and(l >= DENOM_SAFE_LO, l <= DENOM_SAFE_HI),
                             amax < jnp.finfo(f32).max)
        return jnp.where(ok, 0.0, 1.0)

    init = (jnp.zeros((SUBLANES, t), f32), jnp.zeros((HEAD_DIM, t), f32))

    def slot(b):
        return slice(b * t, (b + 1) * t)

    for b in range(cb):
        s_scr[slot(b), :] = scores(b, 0)

    def query_block(i, bad):
        n_chunks = jnp.maximum((i + cb - 1) // cb, 1)
        nxt = jnp.minimum(i + 1, nb - 1)

        def refill(b, kb0, iw):
            if (b + 1) % ga == 0:
                g0 = b + 1 - ga
                r0 = pl.multiple_of((kb0 + g0) * t, ga * t)
                s_scr[g0 * t:(b + 1) * t, :] = jnp.dot(
                    kaug[pl.ds(r0, ga * t), :], w_ref[iw], preferred_element_type=f32)

        def inner(j, carry):
            for b in range(cb):
                carry = absorb(s_scr[slot(b), :], j * cb + b, None, carry)
                refill(b, (j + 1) * cb, i)
            return carry

        carry = lax.fori_loop(0, n_chunks - 1, inner, init)
        for b in range(cb):
            carry = absorb(s_scr[slot(b), :], (n_chunks - 1) * cb + b, None, carry)
            refill(b, 0, nxt)
        l, acc = absorb(own_scores(i), i, None, carry)
        return jnp.maximum(bad, finish(i, jnp.sum(l, axis=0, keepdims=True), acc))

    bad = lax.fori_loop(0, nb, query_block, jnp.zeros((1, t), f32))

    @pl.when(jnp.max(bad) > 0.0)
    def _():
        def redo(i, carry):
            s_own = own_scores(i)
            mx = lax.fori_loop(0, i, lambda kb, m: jnp.maximum(m, fold(scores(kb, i), jnp.max)),
                               fold(s_own, jnp.max))
            top = jnp.max(mx, axis=0, keepdims=True)
            l, acc = lax.fori_loop(0, i, lambda kb, c: absorb(scores(kb, i), kb, top, c), init)
            l, acc = absorb(s_own, i, top, (l, acc))
            finish(i, jnp.sum(l, axis=0, keepdims=True), acc)
            return carry

        lax.fori_loop(0, nb, redo, 0)


def _moba_attention(qz, kv, km, *, cb=8, ga=8):
    s = qz.shape[0]
    d = N_HEADS * HEAD_DIM
    t = MOBA_BLOCK
    h_ = N_HEADS
    nb = s // t
    cb = min(cb, nb)
    ga = min(ga, cb)
    assert nb % cb == 0 and cb % ga == 0
    w_all, vt = _moba_prep(qz, kv, km)
    c_exp = (1.0 / math.sqrt(HEAD_DIM)) * math.log2(math.e)

    def once(shape, index_map):
        return pl.BlockSpec(shape, index_map, pipeline_mode=pl.Buffered(1))

    return pl.pallas_call(
        functools.partial(_moba_kernel, cb=cb, ga=ga, c_exp=c_exp),
        grid=(h_,),
        in_specs=[
            pl.BlockSpec((None, nb, HEAD_DIM + LANES, t), lambda h: (h, 0, 0, 0)),
            once((s, HEAD_DIM), lambda h: (0, h)),
            pl.BlockSpec((None, nb, HEAD_DIM, t), lambda h: (h, 0, 0, 0)),
            once((s, HEAD_DIM), lambda h: (0, h_ + h)),
        ],
        out_specs=pl.BlockSpec((s, HEAD_DIM), lambda h: (0, h)),
        out_shape=jax.ShapeDtypeStruct((s, d), MXU_DTYPE),
        scratch_shapes=[pltpu.VMEM((s, HEAD_DIM + LANES), MXU_DTYPE),
                        pltpu.VMEM((cb * t, t), jnp.float32)],
        compiler_params=pltpu.CompilerParams(
            dimension_semantics=("arbitrary",),
            vmem_limit_bytes=VMEM_LIMIT_BYTES),
        name="moba_attention",
    )(w_all, kv, vt, qz)


def _post_kernel(a_ref, x_ref, p_ref, wo_ref, g_ref, b_ref, wp_ref, wg_ref, o_ref, *, alpha):
    y = jnp.dot(a_ref[...], wo_ref[...], preferred_element_type=jnp.float32)
    xn = _layer_norm(alpha * x_ref[...] + y, g_ref[...], b_ref[...])
    gate = jax.nn.sigmoid(jnp.dot(xn.astype(MXU_DTYPE), wg_ref[...],
                                  preferred_element_type=jnp.float32))
    e = jnp.dot(p_ref[...].astype(MXU_DTYPE), wp_ref[...], preferred_element_type=jnp.float32)
    o_ref[...] = xn + e * gate


def _post(a, x, p, w_out, g, b, w_ple, w_gate, *, alpha, tm=512):
    s, d = x.shape
    pd = p.shape[1]
    tm = min(tm, s)
    assert s % tm == 0

    def const(shape):
        return pl.BlockSpec(shape, lambda m: (0, 0), pipeline_mode=pl.Buffered(1))

    return pl.pallas_call(
        functools.partial(_post_kernel, alpha=alpha),
        grid=(s // tm,),
        in_specs=[
            pl.BlockSpec((tm, d), lambda m: (m, 0)),
            pl.BlockSpec((tm, d), lambda m: (m, 0)),
            pl.BlockSpec((tm, pd), lambda m: (m, 0)),
            const((d, d)),
            const((1, d)),
            const((1, d)),
            const((pd, d)),
            const((d, d)),
        ],
        out_specs=pl.BlockSpec((tm, d), lambda m: (m, 0)),
        out_shape=jax.ShapeDtypeStruct((s, d), jnp.float32),
        compiler_params=pltpu.CompilerParams(
            dimension_semantics=("arbitrary",),
            vmem_limit_bytes=VMEM_LIMIT_BYTES),
        name="post",
    )(a, x, p, w_out, g.reshape(1, d), b.reshape(1, d), w_ple, w_gate)


def _rope_tables(s):
    half = HEAD_DIM // 2
    inv_freq = ROPE_THETA ** (-jnp.arange(half, dtype=jnp.float32) / half)
    ang = jnp.arange(s, dtype=jnp.int32).astype(jnp.float32)[:, None] * inv_freq[None, :]
    cos = jnp.cos(ang)
    sin = jnp.sin(ang)
    return jnp.concatenate([cos, cos], axis=1), jnp.concatenate([-sin, sin], axis=1)


def kernel(x, p, w_in_a, w_out_a, w_kv, ln_kv_g, ln_kv_b, w_in_b, w_out_b, ln_g, ln_b,
           w_ple, w_ple_gate):
    bsz, s, d = x.shape
    depth = p.shape[0]
    n_a = w_in_a.shape[0]
    assert d == N_HEADS * HEAD_DIM and s % MOBA_BLOCK == 0
    alpha = (2.0 * depth) ** 0.25
    cast = lambda w: w.astype(MXU_DTYPE)
    rope = _rope_tables(s)
    nb = s // MOBA_BLOCK
    nbr = -(-nb // SUBLANES) * SUBLANES

    outs = []
    for bi in range(bsz):
        xb = x[bi]
        kv = km = None
        for i in range(depth):
            if i < n_a:
                qkvz = _proj(xb, cast(w_in_a[i]))
                a = _sb_attention(qkvz)
                w_out = w_out_a[i]
            else:
                if i == n_a:
                    kv, km3 = _proj(xb, cast(w_kv), ln=(ln_kv_g, ln_kv_b), rope=rope,
                                    rope_cols=d, with_kmean=True)
                    km = km3.reshape(nb, 2 * d)
                    km = jnp.pad(km, ((0, nbr - nb), (0, 0)))
                qz = _proj(xb, cast(w_in_b[i - n_a]), rope=rope, rope_cols=d)
                a = _moba_attention(qz, kv, km)
                w_out = w_out_b[i - n_a]
            xb = _post(a, xb, p[i, bi], cast(w_out), ln_g[i], ln_b[i], cast(w_ple[i]),
                       cast(w_ple_gate[i]), alpha=alpha)
        outs.append(xb)
    return jnp.stack(outs, axis=0)
```

```python
import functools
import math

import jax
import jax.numpy as jnp
from jax import lax
from jax.experimental import pallas as pl
from jax.experimental.pallas import tpu as pltpu

N_HEADS = 16
HEAD_DIM = 128
ROPE_THETA = 10000.0
MOBA_BLOCK = 256
MOBA_TOPK = 3
LN_EPS = 1e-5
NEG_BIG = -1e30

LANES = 128
SUBLANES = 8
MXU_DTYPE = jnp.bfloat16
VMEM_LIMIT_BYTES = 60 * 1024 * 1024

SB_SKIP_LOG = 104.0

_NT = (((1,), (1,)), ((), ()))


def _layer_norm(x, g, b):
    mu = jnp.mean(x, axis=-1, keepdims=True)
    xc = x - mu
    var = jnp.mean(xc * xc, axis=-1, keepdims=True)
    return xc * lax.rsqrt(var + LN_EPS) * g + b


def _silu(z):
    return z * jax.nn.sigmoid(z)


def _proj_kernel(*refs, ln, rope_tiles, with_kmean, tm, tn):
    refs = list(refs)
    x_ref = refs.pop(0)
    if ln:
        g_ref = refs.pop(0)
        b_ref = refs.pop(0)
    if rope_tiles:
        cos_ref = refs.pop(0)
        sin_ref = refs.pop(0)
    w_ref = refs.pop(0)
    o_ref = refs.pop(0)
    if with_kmean:
        km_ref = refs.pop(0)
    a_scr = refs.pop(0)

    n = pl.program_id(1)

    @pl.when(n == 0)
    def _():
        xv = x_ref[...]
        if ln:
            xv = _layer_norm(xv, g_ref[...], b_ref[...])
        a_scr[...] = xv.astype(a_scr.dtype)

    def matmul(cols=slice(None)):
        return jnp.dot(a_scr[...], w_ref[:, cols], preferred_element_type=jnp.float32)

    def write_plain():
        o_ref[...] = matmul().astype(o_ref.dtype)
        if with_kmean:
            km_ref[...] = jnp.zeros(km_ref.shape, km_ref.dtype)

    def write_rope():
        cos2 = cos_ref[...]
        sin2 = sin_ref[...]
        acc = matmul()
        for j in range(tn // HEAD_DIM):
            t = acc[:, j * HEAD_DIM:(j + 1) * HEAD_DIM]
            r = t * cos2 + pltpu.roll(t, HEAD_DIM // 2, axis=1) * sin2
            o_ref[:, j * HEAD_DIM:(j + 1) * HEAD_DIM] = r.astype(o_ref.dtype)
            if with_kmean:
                for blk in range(tm // MOBA_BLOCK):
                    rows = r[blk * MOBA_BLOCK:(blk + 1) * MOBA_BLOCK, :]
                    km_ref[blk, :, j * HEAD_DIM:(j + 1) * HEAD_DIM] = (
                        jnp.sum(rows, axis=0, keepdims=True) * (1.0 / MOBA_BLOCK))

    if rope_tiles:
        pl.when(n < rope_tiles)(write_rope)
        pl.when(n >= rope_tiles)(write_plain)
    else:
        write_plain()


def _proj(x, w, *, ln=None, rope=None, rope_cols=0, with_kmean=False, tm=1024, tn=1024):
    s, d = x.shape
    n_out = w.shape[1]
    tm = min(tm, s)
    assert s % tm == 0 and n_out % tn == 0 and tn % HEAD_DIM == 0 and rope_cols % tn == 0
    assert tm % MOBA_BLOCK == 0
    rope_tiles = rope_cols // tn
    in_specs = [pl.BlockSpec((tm, d), lambda m, n: (m, 0))]
    args = [x]
    if ln is not None:
        in_specs += [pl.BlockSpec((1, d), lambda m, n: (0, 0))] * 2
        args += [ln[0].reshape(1, d), ln[1].reshape(1, d)]
    if rope_tiles:
        in_specs += [pl.BlockSpec((tm, HEAD_DIM), lambda m, n: (m, 0))] * 2
        args += [rope[0], rope[1]]
    in_specs.append(pl.BlockSpec((d, tn), lambda m, n: (0, n)))
    args.append(w)
    out_shape = [jax.ShapeDtypeStruct((s, n_out), MXU_DTYPE)]
    out_specs = [pl.BlockSpec((tm, tn), lambda m, n: (m, n))]
    if with_kmean:
        out_shape.append(jax.ShapeDtypeStruct((s // MOBA_BLOCK, 1, n_out), jnp.float32))
        out_specs.append(pl.BlockSpec((tm // MOBA_BLOCK, 1, tn), lambda m, n: (m, 0, n)))
    res = pl.pallas_call(
        functools.partial(_proj_kernel, ln=ln is not None, rope_tiles=rope_tiles,
                          with_kmean=with_kmean, tm=tm, tn=tn),
        grid=(s // tm, n_out // tn),
        in_specs=in_specs,
        out_specs=out_specs,
        out_shape=out_shape,
        scratch_shapes=[pltpu.VMEM((tm, d), MXU_DTYPE)],
        compiler_params=pltpu.CompilerParams(
            dimension_semantics=("arbitrary", "arbitrary"),
            vmem_limit_bytes=VMEM_LIMIT_BYTES),
        name="proj",
    )(*args)
    return res if with_kmean else res[0]


def _sb_kernel(q_ref, k_ref, v_ref, z_ref, o_ref, *, t, scale, hp, nq):
    row = lax.broadcasted_iota(jnp.int32, (t, t), 0)
    col = lax.broadcasted_iota(jnp.int32, (t, t), 1)
    tri = (row > col).astype(MXU_DTYPE)
    past = col < row
    pairs = [_SbHead(q_ref, k_ref, v_ref, slice(qq * t, (qq + 1) * t),
                     slice(hh * HEAD_DIM, (hh + 1) * HEAD_DIM), t, scale, tri)
             for qq in range(nq) for hh in range(hp)]
    blocks = [pl.program_id(1) * nq + qq for qq in range(nq) for _ in range(hp)]
    first = [pr.first_two(i, jnp.maximum(i - 1, 0), past, i > 0)
             for pr, i in zip(pairs, blocks)]
    for pr, i, (c, acc) in zip(pairs, blocks, first):
        acc = pr.walk(i - 2, c, acc)
        z = z_ref[pr.rows, pr.cols].astype(jnp.float32)
        o_ref[pr.rows, pr.cols] = (acc * _silu(z)).astype(o_ref.dtype)


class _SbHead:
    def __init__(self, q_ref, k_ref, v_ref, rows, cols, t, scale, tri):
        self.q = q_ref[rows, cols]
        self.k_ref, self.v_ref, self.rows, self.cols = k_ref, v_ref, rows, cols
        self.t, self.scale, self.tri = t, scale, tri

    def logits(self, kb, keep):
        t, tri = self.t, self.tri
        start = pl.multiple_of(kb * t, t)
        s = lax.dot_general(self.q, self.k_ref[pl.ds(start, t), self.cols], _NT,
                            preferred_element_type=jnp.float32) * self.scale
        sp = jnp.log(1.0 + jnp.exp(-jnp.abs(s)))
        log_beta = jnp.minimum(s, 0.0) - sp
        l1m = -jnp.maximum(s, 0.0) - sp
        if keep is not None:
            l1m = jnp.where(keep, l1m, 0.0)
        hi = l1m.astype(MXU_DTYPE)
        lo = (l1m - hi.astype(jnp.float32)).astype(MXU_DTYPE)
        tail = (jnp.dot(hi, tri, preferred_element_type=jnp.float32)
                + jnp.dot(lo, tri, preferred_element_type=jnp.float32))
        return log_beta, tail, jnp.sum(l1m, axis=1, keepdims=True)

    def weighted(self, kb, log_beta, tail, c, keep):
        t = self.t
        start = pl.multiple_of(kb * t, t)
        w = jnp.exp(log_beta + tail + c)
        if keep is not None:
            w = jnp.where(keep, w, 0.0)
        return jnp.dot(w.astype(MXU_DTYPE), self.v_ref[pl.ds(start, t), self.cols],
                       preferred_element_type=jnp.float32)

    def first_two(self, i, prev, past, has_prev):
        lb_d, tail_d, sum_d = self.logits(i, past)
        lb_p, tail_p, sum_p = self.logits(prev, None)
        acc_p = self.weighted(prev, lb_p, tail_p, sum_d, None)
        acc = self.weighted(i, lb_d, tail_d, 0.0, past) + jnp.where(has_prev, acc_p, 0.0)
        return sum_d + jnp.where(has_prev, sum_p, 0.0), acc

    def walk(self, kb0, c, acc):
        def cond(carry):
            kb, c, _ = carry
            return jnp.logical_and(kb >= 0, jnp.max(c) > -SB_SKIP_LOG)

        def body(carry):
            kb, c, acc = carry
            lb, tail, tot = self.logits(kb, None)
            return kb - 1, c + tot, acc + self.weighted(kb, lb, tail, c, None)

        return lax.while_loop(cond, body, (kb0, c, acc))[2]


def _sb_attention(qkvz, *, t=256, hp=2, nq=8):
    s = qkvz.shape[0]
    d = N_HEADS * HEAD_DIM
    nq = min(nq, s // t)
    assert s % (nq * t) == 0 and N_HEADS % hp == 0
    h_ = N_HEADS // hp
    wd = hp * HEAD_DIM
    return pl.pallas_call(
        functools.partial(_sb_kernel, t=t, scale=1.0 / math.sqrt(HEAD_DIM), hp=hp, nq=nq),
        grid=(h_, s // (nq * t)),
        in_specs=[
            pl.BlockSpec((nq * t, wd), lambda h, i: (i, h)),
            pl.BlockSpec((s, wd), lambda h, i: (0, h_ + h)),
            pl.BlockSpec((s, wd), lambda h, i: (0, 2 * h_ + h)),
            pl.BlockSpec((nq * t, wd), lambda h, i: (i, 3 * h_ + h)),
        ],
        out_specs=pl.BlockSpec((nq * t, wd), lambda h, i: (i, h)),
        out_shape=jax.ShapeDtypeStruct((s, d), MXU_DTYPE),
        compiler_params=pltpu.CompilerParams(
            dimension_semantics=("arbitrary", "arbitrary"),
            vmem_limit_bytes=VMEM_LIMIT_BYTES),
        name="sb_attention",
    )(qkvz, qkvz, qkvz, qkvz)


MASK_BIG = 2.0 ** 100
DENOM_SAFE_LO = 2.0 ** -60
DENOM_SAFE_HI = 2.0 ** 60


def _moba_prep_kernel(q_ref, v_ref, km_ref, w_ref, vt_ref, *, g):
    t = MOBA_BLOCK
    f32 = jnp.float32
    nbr = km_ref.shape[0]
    kmb = km_ref[...].astype(MXU_DTYPE)
    blk = lax.broadcasted_iota(jnp.int32, (nbr, t), 0)
    for gg in range(g):
        i = pl.program_id(1) * g + gg
        rows = slice(gg * t, (gg + 1) * t)
        qt = q_ref[rows, :].astype(f32).T.astype(MXU_DTYPE)
        gate = jnp.dot(kmb, qt, preferred_element_type=f32)
        valid = blk < i
        gv = jnp.where(valid, gate, NEG_BIG)
        sel = jnp.zeros((nbr, t), f32)
        for _ in range(MOBA_TOPK):
            mx = jnp.max(gv, axis=0, keepdims=True)
            first = jnp.min(jnp.where(gv == mx, blk, nbr), axis=0, keepdims=True)
            pick = blk == first
            sel = jnp.where(pick, 1.0, sel)
            gv = jnp.where(pick, -jnp.inf, gv)
        bias = jnp.where(jnp.logical_and(valid, sel > 0.0), 0.0, -MASK_BIG)
        w_ref[gg, 0:HEAD_DIM, :] = qt
        w_ref[gg, HEAD_DIM:HEAD_DIM + nbr, :] = bias.astype(w_ref.dtype)
        if nbr < LANES:
            w_ref[gg, HEAD_DIM + nbr:, :] = jnp.zeros((LANES - nbr, t), w_ref.dtype)
        vt_ref[gg] = v_ref[rows, :].astype(f32).T.astype(vt_ref.dtype)


def _moba_prep(qz, kv, km, *, g=16):
    s = qz.shape[0]
    t = MOBA_BLOCK
    h_ = N_HEADS
    nb = s // t
    nbr = km.shape[0]
    g = min(g, nb)
    assert nb % g == 0 and nbr % SUBLANES == 0 and nb <= nbr <= LANES
    return pl.pallas_call(
        functools.partial(_moba_prep_kernel, g=g),
        grid=(h_, nb // g),
        in_specs=[
            pl.BlockSpec((g * t, HEAD_DIM), lambda h, i: (i, h)),
            pl.BlockSpec((g * t, HEAD_DIM), lambda h, i: (i, h_ + h)),
            pl.BlockSpec((nbr, HEAD_DIM), lambda h, i: (0, h)),
        ],
        out_specs=[
            pl.BlockSpec((None, g, HEAD_DIM + LANES, t), lambda h, i: (h, i, 0, 0)),
            pl.BlockSpec((None, g, HEAD_DIM, t), lambda h, i: (h, i, 0, 0)),
        ],
        out_shape=[
            jax.ShapeDtypeStruct((h_, nb, HEAD_DIM + LANES, t), MXU_DTYPE),
            jax.ShapeDtypeStruct((h_, nb, HEAD_DIM, t), MXU_DTYPE),
        ],
        compiler_params=pltpu.CompilerParams(
            dimension_semantics=("arbitrary", "arbitrary"),
            vmem_limit_bytes=VMEM_LIMIT_BYTES),
        name="moba_prep",
    )(qz, kv, km)


def _moba_kernel(w_ref, k_ref, vt_ref, z_ref, o_ref, kaug, s_scr, s_alt, bad_scr, *, cb, ga, c_exp):
    t = MOBA_BLOCK
    nb = vt_ref.shape[0]
    f32 = jnp.float32

    @pl.when(pl.program_id(0) == 0)
    def _():
        lane = lax.broadcasted_iota(jnp.int32, (t, LANES), 1)

        def fill(b, carry):
            r0 = pl.multiple_of(b * t, t)
            kaug[pl.ds(r0, t), HEAD_DIM:HEAD_DIM + LANES] = (lane == b).astype(kaug.dtype)
            return carry

        lax.fori_loop(0, nb, fill, 0)

    kaug[:, 0:HEAD_DIM] = k_ref[...]

    key = lax.broadcasted_iota(jnp.int32, (t, t), 0)
    qry = lax.broadcasted_iota(jnp.int32, (t, t), 1)
    causal = key <= qry

    def fold(x, op):
        return op(x.reshape(t // SUBLANES, SUBLANES, t), axis=0)

    def scores(kb, i):
        r0 = pl.multiple_of(kb * t, t)
        return jnp.dot(kaug[pl.ds(r0, t), :], w_ref[i], preferred_element_type=f32)

    def absorb(s, kb, shift, carry):
        l, acc = carry
        p = jnp.exp2((s if shift is None else s - shift) * c_exp)
        l = l + fold(p, jnp.sum)
        acc = acc + jnp.dot(vt_ref[kb], p.astype(MXU_DTYPE), preferred_element_type=f32)
        return l, acc

    def own_scores(i):
        own = pl.multiple_of(i * t, t)
        s = jnp.dot(k_ref[pl.ds(own, t), :], w_ref[i, 0:HEAD_DIM, :],
                    preferred_element_type=f32)
        return jnp.where(causal, s, -MASK_BIG)

    def finish(i, l8, acc):
        l = jnp.sum(l8, axis=0, keepdims=True)
        o = (acc / l).T
        rows = pl.ds(pl.multiple_of(i * t, t), t)
        o_ref[rows, :] = (o * _silu(z_ref[rows, :].astype(f32))).astype(o_ref.dtype)
        amax = jnp.max(jnp.abs(acc), axis=0, keepdims=True)
        ok = jnp.logical_and(jnp.logical_and(l >= DENOM_SAFE_LO, l <= DENOM_SAFE_HI),
                             amax < jnp.finfo(f32).max)
        bad_scr[...] = jnp.maximum(bad_scr[...], jnp.where(ok, 0.0, 1.0))

    init = (jnp.zeros((SUBLANES, t), f32), jnp.zeros((HEAD_DIM, t), f32))

    def slot(b):
        return slice(b * t, (b + 1) * t)

    for b in range(cb):
        s_scr[slot(b), :] = scores(b, 0)

    def query_block(i):
        n_chunks = jnp.maximum((i + cb - 1) // cb, 1)
        nxt = jnp.minimum(i + 1, nb - 1)

        def refill(b, kb0, iw):
            if (b + 1) % ga == 0:
                g0 = b + 1 - ga
                r0 = pl.multiple_of((kb0 + g0) * t, ga * t)
                s_scr[g0 * t:(b + 1) * t, :] = jnp.dot(
                    kaug[pl.ds(r0, ga * t), :], w_ref[iw], preferred_element_type=f32)

        def inner(j, carry):
            for b in range(cb):
                carry = absorb(s_scr[slot(b), :], j * cb + b, None, carry)
                refill(b, (j + 1) * cb, i)
            return carry

        def chunk_scores(j):
            r0 = pl.multiple_of(j * (cb * t), cb * t)
            return jnp.dot(kaug[pl.ds(r0, cb * t), :], w_ref[i], preferred_element_type=f32)

        def inner_pair(jj, carry):
            j = 2 * jj
            s_alt[...] = chunk_scores(j + 1)
            for b in range(cb):
                carry = absorb(s_scr[slot(b), :], j * cb + b, None, carry)
            s_scr[...] = chunk_scores(j + 2)
            for b in range(cb):
                carry = absorb(s_alt[slot(b), :], (j + 1) * cb + b, None, carry)
            return carry

        pairs = (n_chunks - 1) // 2
        carry = lax.fori_loop(0, pairs, inner_pair, init)
        carry = lax.fori_loop(2 * pairs, n_chunks - 1, inner, carry)
        for b in range(cb):
            carry = absorb(s_scr[slot(b), :], (n_chunks - 1) * cb + b, None, carry)
            refill(b, 0, nxt)
        return absorb(own_scores(i), i, None, carry)

    def trip(i, prev):
        finish(i - 1, *prev)
        return query_block(i)

    bad_scr[...] = jnp.zeros(bad_scr.shape, f32)
    finish(nb - 1, *lax.fori_loop(1, nb, trip, query_block(0)))

    @pl.when(jnp.max(bad_scr[...]) > 0.0)
    def _():
        def redo(i, carry):
            s_own = own_scores(i)
            mx = lax.fori_loop(0, i, lambda kb, m: jnp.maximum(m, fold(scores(kb, i), jnp.max)),
                               fold(s_own, jnp.max))
            top = jnp.max(mx, axis=0, keepdims=True)
            l, acc = lax.fori_loop(0, i, lambda kb, c: absorb(scores(kb, i), kb, top, c), init)
            l, acc = absorb(s_own, i, top, (l, acc))
            finish(i, l, acc)
            return carry

        lax.fori_loop(0, nb, redo, 0)


def _moba_attention(qz, kv, km, *, cb=8, ga=8):
    s = qz.shape[0]
    d = N_HEADS * HEAD_DIM
    t = MOBA_BLOCK
    h_ = N_HEADS
    nb = s // t
    cb = min(cb, nb)
    ga = min(ga, cb)
    assert nb % cb == 0 and cb % ga == 0
    w_all, vt = _moba_prep(qz, kv, km)
    c_exp = (1.0 / math.sqrt(HEAD_DIM)) * math.log2(math.e)

    def once(shape, index_map):
        return pl.BlockSpec(shape, index_map, pipeline_mode=pl.Buffered(1))

    return pl.pallas_call(
        functools.partial(_moba_kernel, cb=cb, ga=ga, c_exp=c_exp),
        grid=(h_,),
        in_specs=[
            pl.BlockSpec((None, nb, HEAD_DIM + LANES, t), lambda h: (h, 0, 0, 0)),
            once((s, HEAD_DIM), lambda h: (0, h)),
            pl.BlockSpec((None, nb, HEAD_DIM, t), lambda h: (h, 0, 0, 0)),
            once((s, HEAD_DIM), lambda h: (0, h_ + h)),
        ],
        out_specs=pl.BlockSpec((s, HEAD_DIM), lambda h: (0, h)),
        out_shape=jax.ShapeDtypeStruct((s, d), MXU_DTYPE),
        scratch_shapes=[pltpu.VMEM((s, HEAD_DIM + LANES), MXU_DTYPE),
                        pltpu.VMEM((cb * t, t), jnp.float32),
                        pltpu.VMEM((cb * t, t), jnp.float32),
                        pltpu.VMEM((SUBLANES, t), jnp.float32)],
        compiler_params=pltpu.CompilerParams(
            dimension_semantics=("arbitrary",),
            vmem_limit_bytes=VMEM_LIMIT_BYTES),
        name="moba_attention",
    )(w_all, kv, vt, qz)


def _post_kernel(a_ref, x_ref, p_ref, wo_ref, g_ref, b_ref, wp_ref, wg_ref, o_ref, *, alpha):
    y = jnp.dot(a_ref[...], wo_ref[...], preferred_element_type=jnp.float32)
    xn = _layer_norm(alpha * x_ref[...] + y, g_ref[...], b_ref[...])
    gate = jax.nn.sigmoid(jnp.dot(xn.astype(MXU_DTYPE), wg_ref[...],
                                  preferred_element_type=jnp.float32))
    e = jnp.dot(p_ref[...].astype(MXU_DTYPE), wp_ref[...], preferred_element_type=jnp.float32)
    o_ref[...] = xn + e * gate


def _post(a, x, p, w_out, g, b, w_ple, w_gate, *, alpha, tm=512):
    s, d = x.shape
    pd = p.shape[1]
    tm = min(tm, s)
    assert s % tm == 0

    def const(shape):
        return pl.BlockSpec(shape, lambda m: (0, 0), pipeline_mode=pl.Buffered(1))

    return pl.pallas_call(
        functools.partial(_post_kernel, alpha=alpha),
        grid=(s // tm,),
        in_specs=[
            pl.BlockSpec((tm, d), lambda m: (m, 0)),
            pl.BlockSpec((tm, d), lambda m: (m, 0)),
            pl.BlockSpec((tm, pd), lambda m: (m, 0)),
            const((d, d)),
            const((1, d)),
            const((1, d)),
            const((pd, d)),
            const((d, d)),
        ],
        out_specs=pl.BlockSpec((tm, d), lambda m: (m, 0)),
        out_shape=jax.ShapeDtypeStruct((s, d), jnp.float32),
        compiler_params=pltpu.CompilerParams(
            dimension_semantics=("arbitrary",),
            vmem_limit_bytes=VMEM_LIMIT_BYTES),
        name="post",
    )(a, x, p, w_out, g.reshape(1, d), b.reshape(1, d), w_ple, w_gate)


def _rope_tables(s):
    half = HEAD_DIM // 2
    inv_freq = ROPE_THETA ** (-jnp.arange(half, dtype=jnp.float32) / half)
    ang = jnp.arange(s, dtype=jnp.int32).astype(jnp.float32)[:, None] * inv_freq[None, :]
    cos = jnp.cos(ang)
    sin = jnp.sin(ang)
    return jnp.concatenate([cos, cos], axis=1), jnp.concatenate([-sin, sin], axis=1)


def kernel(x, p, w_in_a, w_out_a, w_kv, ln_kv_g, ln_kv_b, w_in_b, w_out_b, ln_g, ln_b,
           w_ple, w_ple_gate):
    bsz, s, d = x.shape
    depth = p.shape[0]
    n_a = w_in_a.shape[0]
    assert d == N_HEADS * HEAD_DIM and s % MOBA_BLOCK == 0
    alpha = (2.0 * depth) ** 0.25
    cast = lambda w: w.astype(MXU_DTYPE)
    rope = _rope_tables(s)
    nb = s // MOBA_BLOCK
    nbr = -(-nb // SUBLANES) * SUBLANES

    outs = []
    for bi in range(bsz):
        xb = x[bi]
        kv = km = None
        for i in range(depth):
            if i < n_a:
                qkvz = _proj(xb, cast(w_in_a[i]))
                a = _sb_attention(qkvz)
                w_out = w_out_a[i]
            else:
                if i == n_a:
                    kv, km3 = _proj(xb, cast(w_kv), ln=(ln_kv_g, ln_kv_b), rope=rope,
                                    rope_cols=d, with_kmean=True)
                    km = km3.reshape(nb, 2 * d)
                    km = jnp.pad(km, ((0, nbr - nb), (0, 0)))
                qz = _proj(xb, cast(w_in_b[i - n_a]), rope=rope, rope_cols=d)
                a = _moba_attention(qz, kv, km)
                w_out = w_out_b[i - n_a]
            xb = _post(a, xb, p[i, bi], cast(w_out), ln_g[i], ln_b[i], cast(w_ple[i]),
                       cast(w_ple_gate[i]), alpha=alpha)
        outs.append(xb)
    return jnp.stack(outs, axis=0)
```

```python
import functools
import math

import jax
import jax.numpy as jnp
from jax import lax
from jax.experimental import pallas as pl
from jax.experimental.pallas import tpu as pltpu

N_HEADS = 16
HEAD_DIM = 128
ROPE_THETA = 10000.0
MOBA_BLOCK = 256
MOBA_TOPK = 3
LN_EPS = 1e-5
NEG_BIG = -1e30

LANES = 128
SUBLANES = 8
MXU_DTYPE = jnp.bfloat16
VMEM_LIMIT_BYTES = 60 * 1024 * 1024

SB_SKIP_LOG = 104.0

_NT = (((1,), (1,)), ((), ()))


def _layer_norm(x, g, b):
    mu = jnp.mean(x, axis=-1, keepdims=True)
    xc = x - mu
    var = jnp.mean(xc * xc, axis=-1, keepdims=True)
    return xc * lax.rsqrt(var + LN_EPS) * g + b


def _silu(z):
    return z * jax.nn.sigmoid(z)


def _proj_kernel(*refs, ln, rope_tiles, with_kmean, tm, tn):
    refs = list(refs)
    x_ref = refs.pop(0)
    if ln:
        g_ref = refs.pop(0)
        b_ref = refs.pop(0)
    if rope_tiles:
        cos_ref = refs.pop(0)
        sin_ref = refs.pop(0)
    w_ref = refs.pop(0)
    o_ref = refs.pop(0)
    if with_kmean:
        km_ref = refs.pop(0)
    a_scr = refs.pop(0)

    n = pl.program_id(1)

    @pl.when(n == 0)
    def _():
        xv = x_ref[...]
        if ln:
            xv = _layer_norm(xv, g_ref[...], b_ref[...])
        a_scr[...] = xv.astype(a_scr.dtype)

    def matmul(cols=slice(None)):
        return jnp.dot(a_scr[...], w_ref[:, cols], preferred_element_type=jnp.float32)

    def write_plain():
        o_ref[...] = matmul().astype(o_ref.dtype)
        if with_kmean:
            km_ref[...] = jnp.zeros(km_ref.shape, km_ref.dtype)

    def write_rope():
        cos2 = cos_ref[...]
        sin2 = sin_ref[...]
        acc = matmul()
        for j in range(tn // HEAD_DIM):
            t = acc[:, j * HEAD_DIM:(j + 1) * HEAD_DIM]
            r = t * cos2 + pltpu.roll(t, HEAD_DIM // 2, axis=1) * sin2
            o_ref[:, j * HEAD_DIM:(j + 1) * HEAD_DIM] = r.astype(o_ref.dtype)
            if with_kmean:
                for blk in range(tm // MOBA_BLOCK):
                    rows = r[blk * MOBA_BLOCK:(blk + 1) * MOBA_BLOCK, :]
                    km_ref[blk, :, j * HEAD_DIM:(j + 1) * HEAD_DIM] = (
                        jnp.sum(rows, axis=0, keepdims=True) * (1.0 / MOBA_BLOCK))

    if rope_tiles:
        pl.when(n < rope_tiles)(write_rope)
        pl.when(n >= rope_tiles)(write_plain)
    else:
        write_plain()


def _proj(x, w, *, ln=None, rope=None, rope_cols=0, with_kmean=False, tm=1024, tn=1024):
    s, d = x.shape
    n_out = w.shape[1]
    tm = min(tm, s)
    assert s % tm == 0 and n_out % tn == 0 and tn % HEAD_DIM == 0 and rope_cols % tn == 0
    assert tm % MOBA_BLOCK == 0
    rope_tiles = rope_cols // tn
    in_specs = [pl.BlockSpec((tm, d), lambda m, n: (m, 0))]
    args = [x]
    if ln is not None:
        in_specs += [pl.BlockSpec((1, d), lambda m, n: (0, 0))] * 2
        args += [ln[0].reshape(1, d), ln[1].reshape(1, d)]
    if rope_tiles:
        in_specs += [pl.BlockSpec((tm, HEAD_DIM), lambda m, n: (m, 0))] * 2
        args += [rope[0], rope[1]]
    in_specs.append(pl.BlockSpec((d, tn), lambda m, n: (0, n)))
    args.append(w)
    out_shape = [jax.ShapeDtypeStruct((s, n_out), MXU_DTYPE)]
    out_specs = [pl.BlockSpec((tm, tn), lambda m, n: (m, n))]
    if with_kmean:
        out_shape.append(jax.ShapeDtypeStruct((s // MOBA_BLOCK, 1, n_out), jnp.float32))
        out_specs.append(pl.BlockSpec((tm // MOBA_BLOCK, 1, tn), lambda m, n: (m, 0, n)))
    res = pl.pallas_call(
        functools.partial(_proj_kernel, ln=ln is not None, rope_tiles=rope_tiles,
                          with_kmean=with_kmean, tm=tm, tn=tn),
        grid=(s // tm, n_out // tn),
        in_specs=in_specs,
        out_specs=out_specs,
        out_shape=out_shape,
        scratch_shapes=[pltpu.VMEM((tm, d), MXU_DTYPE)],
        compiler_params=pltpu.CompilerParams(
            dimension_semantics=("arbitrary", "arbitrary"),
            vmem_limit_bytes=VMEM_LIMIT_BYTES),
        name="proj",
    )(*args)
    return res if with_kmean else res[0]


def _sb_kernel(q_ref, k_ref, v_ref, z_ref, o_ref, *, t, scale, hp, nq):
    row = lax.broadcasted_iota(jnp.int32, (t, t), 0)
    col = lax.broadcasted_iota(jnp.int32, (t, t), 1)
    tri = (row > col).astype(MXU_DTYPE)
    past = col < row
    pairs = [_SbHead(q_ref, k_ref, v_ref, slice(qq * t, (qq + 1) * t),
                     slice(hh * HEAD_DIM, (hh + 1) * HEAD_DIM), t, scale, tri)
             for qq in range(nq) for hh in range(hp)]
    blocks = [pl.program_id(1) * nq + qq for qq in range(nq) for _ in range(hp)]
    first = [pr.first_two(i, jnp.maximum(i - 1, 0), past, i > 0)
             for pr, i in zip(pairs, blocks)]
    for pr, i, (c, acc) in zip(pairs, blocks, first):
        acc = pr.walk(i - 2, c, acc)
        z = z_ref[pr.rows, pr.cols].astype(jnp.float32)
        o_ref[pr.rows, pr.cols] = (acc * _silu(z)).astype(o_ref.dtype)


class _SbHead:
    def __init__(self, q_ref, k_ref, v_ref, rows, cols, t, scale, tri):
        self.q = q_ref[rows, cols]
        self.k_ref, self.v_ref, self.rows, self.cols = k_ref, v_ref, rows, cols
        self.t, self.scale, self.tri = t, scale, tri

    def logits(self, kb, keep):
        t, tri = self.t, self.tri
        start = pl.multiple_of(kb * t, t)
        s = lax.dot_general(self.q, self.k_ref[pl.ds(start, t), self.cols], _NT,
                            preferred_element_type=jnp.float32) * self.scale
        sp = jnp.log(1.0 + jnp.exp(-jnp.abs(s)))
        log_beta = jnp.minimum(s, 0.0) - sp
        l1m = -jnp.maximum(s, 0.0) - sp
        if keep is not None:
            l1m = jnp.where(keep, l1m, 0.0)
        hi = l1m.astype(MXU_DTYPE)
        lo = (l1m - hi.astype(jnp.float32)).astype(MXU_DTYPE)
        tail = (jnp.dot(hi, tri, preferred_element_type=jnp.float32)
                + jnp.dot(lo, tri, preferred_element_type=jnp.float32))
        return log_beta, tail, jnp.sum(l1m, axis=1, keepdims=True)

    def weighted(self, kb, log_beta, tail, c, keep):
        t = self.t
        start = pl.multiple_of(kb * t, t)
        w = jnp.exp(log_beta + tail + c)
        if keep is not None:
            w = jnp.where(keep, w, 0.0)
        return jnp.dot(w.astype(MXU_DTYPE), self.v_ref[pl.ds(start, t), self.cols],
                       preferred_element_type=jnp.float32)

    def first_two(self, i, prev, past, has_prev):
        lb_d, tail_d, sum_d = self.logits(i, past)
        lb_p, tail_p, sum_p = self.logits(prev, None)
        acc_p = self.weighted(prev, lb_p, tail_p, sum_d, None)
        acc = self.weighted(i, lb_d, tail_d, 0.0, past) + jnp.where(has_prev, acc_p, 0.0)
        return sum_d + jnp.where(has_prev, sum_p, 0.0), acc

    def walk(self, kb0, c, acc):
        def cond(carry):
            kb, c, _ = carry
            return jnp.logical_and(kb >= 0, jnp.max(c) > -SB_SKIP_LOG)

        def body(carry):
            kb, c, acc = carry
            lb, tail, tot = self.logits(kb, None)
            return kb - 1, c + tot, acc + self.weighted(kb, lb, tail, c, None)

        return lax.while_loop(cond, body, (kb0, c, acc))[2]


def _sb_attention(qkvz, *, t=256, hp=2, nq=8):
    s = qkvz.shape[0]
    d = N_HEADS * HEAD_DIM
    nq = min(nq, s // t)
    assert s % (nq * t) == 0 and N_HEADS % hp == 0
    h_ = N_HEADS // hp
    wd = hp * HEAD_DIM
    return pl.pallas_call(
        functools.partial(_sb_kernel, t=t, scale=1.0 / math.sqrt(HEAD_DIM), hp=hp, nq=nq),
        grid=(h_, s // (nq * t)),
        in_specs=[
            pl.BlockSpec((nq * t, wd), lambda h, i: (i, h)),
            pl.BlockSpec((s, wd), lambda h, i: (0, h_ + h)),
            pl.BlockSpec((s, wd), lambda h, i: (0, 2 * h_ + h)),
            pl.BlockSpec((nq * t, wd), lambda h, i: (i, 3 * h_ + h)),
        ],
        out_specs=pl.BlockSpec((nq * t, wd), lambda h, i: (i, h)),
        out_shape=jax.ShapeDtypeStruct((s, d), MXU_DTYPE),
        compiler_params=pltpu.CompilerParams(
            dimension_semantics=("arbitrary", "arbitrary"),
            vmem_limit_bytes=VMEM_LIMIT_BYTES),
        name="sb_attention",
    )(qkvz, qkvz, qkvz, qkvz)


MASK_BIG = 2.0 ** 100
DENOM_SAFE_LO = 2.0 ** -60
DENOM_SAFE_HI = 2.0 ** 60


def _moba_prep_kernel(q_ref, v_ref, km_ref, w_ref, vt_ref, *, g):
    t = MOBA_BLOCK
    f32 = jnp.float32
    nbr = km_ref.shape[0]
    kmb = km_ref[...].astype(MXU_DTYPE)
    blk = lax.broadcasted_iota(jnp.int32, (nbr, t), 0)
    for gg in range(g):
        i = pl.program_id(1) * g + gg
        rows = slice(gg * t, (gg + 1) * t)
        qt = q_ref[rows, :].astype(f32).T.astype(MXU_DTYPE)
        gate = jnp.dot(kmb, qt, preferred_element_type=f32)
        valid = blk < i
        gv = jnp.where(valid, gate, NEG_BIG)
        sel = jnp.zeros((nbr, t), f32)
        for _ in range(MOBA_TOPK):
            mx = jnp.max(gv, axis=0, keepdims=True)
            first = jnp.min(jnp.where(gv == mx, blk, nbr), axis=0, keepdims=True)
            pick = blk == first
            sel = jnp.where(pick, 1.0, sel)
            gv = jnp.where(pick, -jnp.inf, gv)
        bias = jnp.where(jnp.logical_and(valid, sel > 0.0), 0.0, -MASK_BIG)
        w_ref[gg, 0:HEAD_DIM, :] = qt
        w_ref[gg, HEAD_DIM:HEAD_DIM + nbr, :] = bias.astype(w_ref.dtype)
        if nbr < LANES:
            w_ref[gg, HEAD_DIM + nbr:, :] = jnp.zeros((LANES - nbr, t), w_ref.dtype)
        vt_ref[gg] = v_ref[rows, :].astype(f32).T.astype(vt_ref.dtype)


def _moba_prep(qz, kv, km, *, g=16):
    s = qz.shape[0]
    t = MOBA_BLOCK
    h_ = N_HEADS
    nb = s // t
    nbr = km.shape[0]
    g = min(g, nb)
    assert nb % g == 0 and nbr % SUBLANES == 0 and nb <= nbr <= LANES
    return pl.pallas_call(
        functools.partial(_moba_prep_kernel, g=g),
        grid=(h_, nb // g),
        in_specs=[
            pl.BlockSpec((g * t, HEAD_DIM), lambda h, i: (i, h)),
            pl.BlockSpec((g * t, HEAD_DIM), lambda h, i: (i, h_ + h)),
            pl.BlockSpec((nbr, HEAD_DIM), lambda h, i: (0, h)),
        ],
        out_specs=[
            pl.BlockSpec((None, g, HEAD_DIM + LANES, t), lambda h, i: (h, i, 0, 0)),
            pl.BlockSpec((None, g, HEAD_DIM, t), lambda h, i: (h, i, 0, 0)),
        ],
        out_shape=[
            jax.ShapeDtypeStruct((h_, nb, HEAD_DIM + LANES, t), MXU_DTYPE),
            jax.ShapeDtypeStruct((h_, nb, HEAD_DIM, t), MXU_DTYPE),
        ],
        compiler_params=pltpu.CompilerParams(
            dimension_semantics=("arbitrary", "arbitrary"),
            vmem_limit_bytes=VMEM_LIMIT_BYTES),
        name="moba_prep",
    )(qz, kv, km)


def _moba_kernel(w_ref, k_ref, vt_ref, z_ref, o_ref, kaug, s_scr, *, cb, ga, c_exp):
    t = MOBA_BLOCK
    nb = vt_ref.shape[0]
    f32 = jnp.float32

    @pl.when(pl.program_id(0) == 0)
    def _():
        lane = lax.broadcasted_iota(jnp.int32, (t, LANES), 1)

        def fill(b, carry):
            r0 = pl.multiple_of(b * t, t)
            kaug[pl.ds(r0, t), HEAD_DIM:HEAD_DIM + LANES] = (lane == b).astype(kaug.dtype)
            return carry

        lax.fori_loop(0, nb, fill, 0)

    kaug[:, 0:HEAD_DIM] = k_ref[...]

    key = lax.broadcasted_iota(jnp.int32, (t, t), 0)
    qry = lax.broadcasted_iota(jnp.int32, (t, t), 1)
    causal = key <= qry

    def fold(x, op):
        return op(x.reshape(t // SUBLANES, SUBLANES, t), axis=0)

    def scores(kb, i):
        r0 = pl.multiple_of(kb * t, t)
        return jnp.dot(kaug[pl.ds(r0, t), :], w_ref[i], preferred_element_type=f32)

    def absorb(s, kb, shift, carry):
        l, acc = carry
        p = jnp.exp2((s if shift is None else s - shift) * c_exp)
        l = l + fold(p, jnp.sum)
        acc = acc + jnp.dot(vt_ref[kb], p.astype(MXU_DTYPE), preferred_element_type=f32)
        return l, acc

    def own_scores(i):
        own = pl.multiple_of(i * t, t)
        s = jnp.dot(k_ref[pl.ds(own, t), :], w_ref[i, 0:HEAD_DIM, :],
                    preferred_element_type=f32)
        return jnp.where(causal, s, -MASK_BIG)

    def finish(i, l, acc):
        o = (acc / l).T
        rows = pl.ds(pl.multiple_of(i * t, t), t)
        o_ref[rows, :] = (o * _silu(z_ref[rows, :].astype(f32))).astype(o_ref.dtype)
        amax = jnp.max(jnp.abs(acc), axis=0, keepdims=True)
        ok = jnp.logical_and(jnp.logical_and(l >= DENOM_SAFE_LO, l <= DENOM_SAFE_HI),
                             amax < jnp.finfo(f32).max)
        return jnp.where(ok, 0.0, 1.0)

    init = (jnp.zeros((SUBLANES, t), f32), jnp.zeros((HEAD_DIM, t), f32))

    def slot(b):
        return slice(b * t, (b + 1) * t)

    for b in range(cb):
        s_scr[slot(b), :] = scores(b, 0)

    def query_block(i, bad):
        n_chunks = jnp.maximum((i + cb - 1) // cb, 1)
        nxt = jnp.minimum(i + 1, nb - 1)

        def refill(b, kb0, iw):
            if (b + 1) % ga == 0:
                g0 = b + 1 - ga
                r0 = pl.multiple_of((kb0 + g0) * t, ga * t)
                s_scr[g0 * t:(b + 1) * t, :] = jnp.dot(
                    kaug[pl.ds(r0, ga * t), :], w_ref[iw], preferred_element_type=f32)

        def inner(j, carry):
            for b in range(cb):
                carry = absorb(s_scr[slot(b), :], j * cb + b, None, carry)
                refill(b, (j + 1) * cb, i)
            return carry

        carry = lax.fori_loop(0, n_chunks - 1, inner, init)
        for b in range(cb):
            carry = absorb(s_scr[slot(b), :], (n_chunks - 1) * cb + b, None, carry)
            refill(b, 0, nxt)
        l, acc = absorb(own_scores(i), i, None, carry)
        return jnp.maximum(bad, finish(i, jnp.sum(l, axis=0, keepdims=True), acc))

    bad = lax.fori_loop(0, nb, query_block, jnp.zeros((1, t), f32))

    @pl.when(jnp.max(bad) > 0.0)
    def _():
        def redo(i, carry):
            s_own = own_scores(i)
            mx = lax.fori_loop(0, i, lambda kb, m: jnp.maximum(m, fold(scores(kb, i), jnp.max)),
                               fold(s_own, jnp.max))
            top = jnp.max(mx, axis=0, keepdims=True)
            l, acc = lax.fori_loop(0, i, lambda kb, c: absorb(scores(kb, i), kb, top, c), init)
            l, acc = absorb(s_own, i, top, (l, acc))
            finish(i, jnp.sum(l, axis=0, keepdims=True), acc)
            return carry

        lax.fori_loop(0, nb, redo, 0)


def _moba_attention(qz, kv, km, *, cb=8, ga=8):
    s = qz.shape[0]
    d = N_HEADS * HEAD_DIM
    t = MOBA_BLOCK
    h_ = N_HEADS
    nb = s // t
    cb = min(cb, nb)
    ga = min(ga, cb)
    assert nb % cb == 0 and cb % ga == 0
    w_all, vt = _moba_prep(qz, kv, km)
    c_exp = (1.0 / math.sqrt(HEAD_DIM)) * math.log2(math.e)

    def once(shape, index_map):
        return pl.BlockSpec(shape, index_map, pipeline_mode=pl.Buffered(1))

    return pl.pallas_call(
        functools.partial(_moba_kernel, cb=cb, ga=ga, c_exp=c_exp),
        grid=(h_,),
        in_specs=[
            pl.BlockSpec((None, nb, HEAD_DIM + LANES, t), lambda h: (h, 0, 0, 0)),
            once((s, HEAD_DIM), lambda h: (0, h)),
            pl.BlockSpec((None, nb, HEAD_DIM, t), lambda h: (h, 0, 0, 0)),
            once((s, HEAD_DIM), lambda h: (0, h_ + h)),
        ],
        out_specs=pl.BlockSpec((s, HEAD_DIM), lambda h: (0, h)),
        out_shape=jax.ShapeDtypeStruct((s, d), MXU_DTYPE),
        scratch_shapes=[pltpu.VMEM((s, HEAD_DIM + LANES), MXU_DTYPE),
                        pltpu.VMEM((cb * t, t), jnp.float32)],
        compiler_params=pltpu.CompilerParams(
            dimension_semantics=("arbitrary",),
            vmem_limit_bytes=VMEM_LIMIT_BYTES),
        name="moba_attention",
    )(w_all, kv, vt, qz)


def _post_kernel(a_ref, x_ref, p_ref, wo_ref, g_ref, b_ref, wp_ref, wg_ref, o_ref, *, alpha):
    y = jnp.dot(a_ref[...], wo_ref[...], preferred_element_type=jnp.float32)
    xn = _layer_norm(alpha * x_ref[...] + y, g_ref[...], b_ref[...])
    gate = jax.nn.sigmoid(jnp.dot(xn.astype(MXU_DTYPE), wg_ref[...],
                                  preferred_element_type=jnp.float32))
    e = jnp.dot(p_ref[...].astype(MXU_DTYPE), wp_ref[...], preferred_element_type=jnp.float32)
    o_ref[...] = xn + e * gate


def _post(a, x, p, w_out, g, b, w_ple, w_gate, *, alpha, tm=512):
    s, d = x.shape
    pd = p.shape[1]
    tm = min(tm, s)
    assert s % tm == 0

    def const(shape):
        return pl.BlockSpec(shape, lambda m: (0, 0), pipeline_mode=pl.Buffered(1))

    return pl.pallas_call(
        functools.partial(_post_kernel, alpha=alpha),
        grid=(s // tm,),
        in_specs=[
            pl.BlockSpec((tm, d), lambda m: (m, 0)),
            pl.BlockSpec((tm, d), lambda m: (m, 0)),
            pl.BlockSpec((tm, pd), lambda m: (m, 0)),
            const((d, d)),
            const((1, d)),
            const((1, d)),
            const((pd, d)),
            const((d, d)),
        ],
        out_specs=pl.BlockSpec((tm, d), lambda m: (m, 0)),
        out_shape=jax.ShapeDtypeStruct((s, d), jnp.float32),
        compiler_params=pltpu.CompilerParams(
            dimension_semantics=("arbitrary",),
            vmem_limit_bytes=VMEM_LIMIT_BYTES),
        name="post",
    )(a, x, p, w_out, g.reshape(1, d), b.reshape(1, d), w_ple, w_gate)


def _rope_tables(s):
    half = HEAD_DIM // 2
    inv_freq = ROPE_THETA ** (-jnp.arange(half, dtype=jnp.float32) / half)
    ang = jnp.arange(s, dtype=jnp.int32).astype(jnp.float32)[:, None] * inv_freq[None, :]
    cos = jnp.cos(ang)
    sin = jnp.sin(ang)
    return jnp.concatenate([cos, cos], axis=1), jnp.concatenate([-sin, sin], axis=1)


def kernel(x, p, w_in_a, w_out_a, w_kv, ln_kv_g, ln_kv_b, w_in_b, w_out_b, ln_g, ln_b,
           w_ple, w_ple_gate):
    bsz, s, d = x.shape
    depth = p.shape[0]
    n_a = w_in_a.shape[0]
    assert d == N_HEADS * HEAD_DIM and s % MOBA_BLOCK == 0
    alpha = (2.0 * depth) ** 0.25
    cast = lambda w: w.astype(MXU_DTYPE)
    rope = _rope_tables(s)
    nb = s // MOBA_BLOCK
    nbr = -(-nb // SUBLANES) * SUBLANES

    outs = []
    for bi in range(bsz):
        xb = x[bi]
        kv = km = None
        for i in range(depth):
            if i < n_a:
                qkvz = _proj(xb, cast(w_in_a[i]))
                a = _sb_attention(qkvz)
                w_out = w_out_a[i]
            else:
                if i == n_a:
                    kv, km3 = _proj(xb, cast(w_kv), ln=(ln_kv_g, ln_kv_b), rope=rope,
                                    rope_cols=d, with_kmean=True)
                    km = km3.reshape(nb, 2 * d)
                    km = jnp.pad(km, ((0, nbr - nb), (0, 0)))
                qz = _proj(xb, cast(w_in_b[i - n_a]), rope=rope, rope_cols=d)
                a = _moba_attention(qz, kv, km)
                w_out = w_out_b[i - n_a]
            xb = _post(a, xb, p[i, bi], cast(w_out), ln_g[i], ln_b[i], cast(w_ple[i]),
                       cast(w_ple_gate[i]), alpha=alpha)
        outs.append(xb)
    return jnp.stack(outs, axis=0)
```
